```python
import jax, jax.numpy as jnp
from jax import lax
import numpy as np

D_MODEL = 1024
BATCH = 16
SEQ = 256
DEPTH = 4
DEC_BATCH = 8
DEC_SEQ = 4096
PAST_LEN = 256

GRID_W = 64
N_MIXERS = 2
N_RWKV = (DEPTH + 1) // 2
N_NA = DEPTH // 2
RWKV_HEAD = 64
RWKV_HEADS = D_MODEL // RWKV_HEAD
DECAY_LORA = 64
AAA_LORA = 64
GATE_LORA = 128
NA_HEAD = 64
NA_HEADS = D_MODEL // NA_HEAD
WIN_ROWS = 8
WIN_COLS = 16
Q_COL_BLOCK = 16
K_COL_BAND = 32
N_COL_BLOCKS = GRID_W // Q_COL_BLOCK
D_FF = 4 * D_MODEL
NORM_EPS = 1e-6
GN_EPS = 64e-5
ATTN_SCALE = NA_HEAD ** -0.5
NEG_BIG = -1e30

kernel_name = 'hybrid_rwkv7_natten_diffusion_step'


def rms_norm(x, g):
    x32 = x.astype(jnp.float32)
    y = x32 * lax.rsqrt(jnp.mean(x32 * x32, axis=-1, keepdims=True) + NORM_EPS)
    return (y * g.astype(jnp.float32)).astype(x.dtype)


def ada_mod(cond, w, b):
    m = jax.nn.silu(cond) @ w + b
    return jnp.split(m, 6, axis=-1)


def modulate(x, shift, scale):
    return x * (1 + scale[:, None, :]) + shift[:, None, :]


def sq_relu_mlp(h, w1, w2):
    return jnp.square(jax.nn.relu(h @ w1)) @ w2


def centred_token_shift(x):
    zero = jnp.zeros_like(x[:, :1])
    prev = jnp.concatenate([zero, x[:, :-1]], axis=1)
    nxt = jnp.concatenate([x[:, 1:], zero], axis=1)
    return 0.5 * (prev + nxt)


def delta_rule_scan(r, w, k, v, kk, a, s0, reverse):
    def step(S, inp):
        r_t, w_t, k_t, v_t, kk_t, a_t = inp
        sa = jnp.einsum('bhvk,bhk->bhv', S, -kk_t)
        S_new = (S * w_t[:, :, None, :] + sa[..., None] * (kk_t * a_t)[:, :, None, :]
                 + v_t[..., None] * k_t[:, :, None, :]).astype(S.dtype)
        y = jnp.einsum('bhvk,bhk->bhv', S_new, r_t)
        return S_new, y
    xs = tuple(jnp.moveaxis(t, 1, 0) for t in (r, w, k, v, kk, a))
    S_fin, ys = lax.scan(step, s0, xs, reverse=reverse)
    return jnp.moveaxis(ys, 0, 1), S_fin


def rwkv7_mixer(h, s0, mu, w_rkv, w_o, w0, w1, w2, a0, a1, a2, g1, g2, k_k, k_a, r_k, ln_w, ln_b):
    B, T, D = h.shape
    H, K = RWKV_HEADS, RWKV_HEAD
    delta = centred_token_shift(h) - h
    xs = h[None] + delta[None] * mu[:, None, None, :]
    rkv = jnp.einsum('ibtd,ide->ibte', xs[:3], w_rkv)
    r, k, v = rkv[0], rkv[1], rkv[2]
    xw, xa, xg = xs[3], xs[4], xs[5]
    g = jax.nn.sigmoid(xg @ g1) @ g2
    heads = lambda t: t.reshape(B, T, H, K)
    kk = heads(k * k_k).astype(jnp.float32)
    kk = (kk * lax.rsqrt(jnp.sum(kk * kk, axis=-1, keepdims=True) + 1e-12)).astype(h.dtype)
    r_h, v_h = heads(r), heads(v)
    ys, finals, bonuses = [], [], []
    for d in range(2):
        z = w0[d] + jnp.tanh(xw @ w1[d]) @ w2[d]
        decay = jnp.exp(-jnp.exp(-jax.nn.softplus(-z) - 0.5))
        a = jax.nn.sigmoid(a0[d] + (xa @ a1[d]) @ a2[d])
        k_d = heads(k * (1 + (a - 1) * k_a))
        y_d, s_d = delta_rule_scan(r_h, heads(decay), k_d, v_h, kk, heads(a), s0[:, d], reverse=(d == 1))
        ys.append(y_d)
        finals.append(s_d)
        bonuses.append(jnp.sum(r_h * k_d * r_k, axis=-1, keepdims=True) * v_h)
    y = (ys[0] + ys[1]).astype(jnp.float32)
    mean = jnp.mean(y, axis=-1, keepdims=True)
    var = jnp.mean(jnp.square(y - mean), axis=-1, keepdims=True)
    yn = ((y - mean) * lax.rsqrt(var + GN_EPS)).reshape(B, T, D) * ln_w + ln_b
    o = (yn + (bonuses[0] + bonuses[1]).reshape(B, T, D)) * g
    return o.astype(h.dtype) @ w_o, jnp.stack(finals, axis=1)


def na_qkv(h, w_qkv, q_g, k_g):
    B, T, _ = h.shape
    qkv = (h @ w_qkv).reshape(B, T, 3, NA_HEADS, NA_HEAD)
    q = rms_norm(qkv[:, :, 0], q_g).transpose(0, 2, 1, 3)
    k = rms_norm(qkv[:, :, 1], k_g).transpose(0, 2, 1, 3)
    v = qkv[:, :, 2].transpose(0, 2, 1, 3)
    return q, k, v


def na_context(h, w_qkv, w_o, q_g, k_g):
    B, T, D = h.shape
    q, k, v = na_qkv(h, w_qkv, q_g, k_g)
    s = jnp.einsum('bhqd,bhkd->bhqk', q, k).astype(jnp.float32) * ATTN_SCALE
    p = jax.nn.softmax(s, axis=-1).astype(v.dtype)
    o = jnp.einsum('bhqk,bhkd->bhqd', p, v)
    return o.transpose(0, 2, 1, 3).reshape(B, T, D) @ w_o, k, v


def neighbourhood_tables():
    cb = np.arange(N_COL_BLOCKS)
    band_start = np.clip(cb * Q_COL_BLOCK - WIN_COLS // 2, 0, GRID_W - K_COL_BAND)
    band_idx = band_start[:, None] + np.arange(K_COL_BAND)[None, :]
    q_col = cb[:, None] * Q_COL_BLOCK + np.arange(Q_COL_BLOCK)[None, :]
    win_start = np.clip(q_col - WIN_COLS // 2, 0, GRID_W - WIN_COLS)
    key_col = band_idx[:, None, :]
    valid = (key_col >= win_start[..., None]) & (key_col < win_start[..., None] + WIN_COLS)
    col_off = np.clip(key_col - q_col[..., None] + WIN_COLS - 1, 0, 2 * WIN_COLS - 2)
    return band_idx.astype(np.int32), valid, col_off.astype(np.int32)


def na_latent(h, k_ctx, v_ctx, w_qkv, w_o, q_g, k_g, rpb):
    B, T, D = h.shape
    rows = T // GRID_W
    wr = min(WIN_ROWS, rows)
    q, k, v = na_qkv(h, w_qkv, q_g, k_g)
    grid = lambda t: t.reshape(B, NA_HEADS, rows, GRID_W, NA_HEAD)
    qg, kg, vg = grid(q), grid(k), grid(v)
    band_np, valid_np, col_off_np = neighbourhood_tables()
    band_idx = jnp.asarray(band_np)
    valid = jnp.asarray(valid_np)[:, :, None, :]
    col_off = jnp.asarray(col_off_np)
    n_win = wr * K_COL_BAND

    def row_block(r):
        rs = jnp.clip(r - WIN_ROWS // 2, 0, rows - wr)
        q_r = lax.dynamic_index_in_dim(qg, r, axis=2, keepdims=False).reshape(
            B, NA_HEADS, N_COL_BLOCKS, Q_COL_BLOCK, NA_HEAD)
        k_r = jnp.take(lax.dynamic_slice_in_dim(kg, rs, wr, axis=2), band_idx, axis=3)
        v_r = jnp.take(lax.dynamic_slice_in_dim(vg, rs, wr, axis=2), band_idx, axis=3)
        row_off = rs + jnp.arange(wr) - r + WIN_ROWS - 1
        bias = jnp.take(jnp.take(rpb, row_off, axis=1), col_off, axis=2)
        bias = bias.transpose(0, 2, 3, 1, 4).astype(jnp.float32)
        s_win = jnp.einsum('bhnqd,bhrnkd->bhnqrk', q_r, k_r).astype(jnp.float32) * ATTN_SCALE + bias
        s_win = jnp.where(valid, s_win, NEG_BIG)
        s_ctx = jnp.einsum('bhnqd,bhld->bhnql', q_r, k_ctx).astype(jnp.float32) * ATTN_SCALE
        s = jnp.concatenate([s_win.reshape(B, NA_HEADS, N_COL_BLOCKS, Q_COL_BLOCK, n_win), s_ctx], axis=-1)
        p = jax.nn.softmax(s, axis=-1).astype(v.dtype)
        p_win = p[..., :n_win].reshape(B, NA_HEADS, N_COL_BLOCKS, Q_COL_BLOCK, wr, K_COL_BAND)
        p_ctx = p[..., n_win:]
        o = (jnp.einsum('bhnqrk,bhrnkd->bhnqd', p_win, v_r)
             + jnp.einsum('bhnql,bhld->bhnqd', p_ctx, v_ctx))
        return o.reshape(B, NA_HEADS, GRID_W, NA_HEAD)

    o = lax.map(row_block, jnp.arange(rows))
    o = o.transpose(1, 0, 3, 2, 4).reshape(B, T, D)
    return o @ w_o


def setup_inputs(seed: int = 0) -> dict:
    key = jax.random.key(seed)
    ks = iter(jax.random.split(key, 40))
    nrm = lambda shape, scale: jax.random.normal(next(ks), shape, jnp.float32) * scale
    D, H, K = D_MODEL, RWKV_HEADS, RWKV_HEAD
    return {
        'x_prompt': nrm((BATCH, SEQ, D), 1.0),
        'x_sample': nrm((DEC_BATCH, DEC_SEQ, D), 1.0),
        'state_rwkv': nrm((DEC_BATCH, N_RWKV, 2, H, K, K), 0.5),
        'cache_na_k': nrm((DEC_BATCH, N_NA, NA_HEADS, PAST_LEN, NA_HEAD), 1.0),
        'cache_na_v': nrm((DEC_BATCH, N_NA, NA_HEADS, PAST_LEN, NA_HEAD), 1.0),
        'c': nrm((DEC_BATCH, D), 1.0),
        'c_ctx': nrm((D,), 1.0),
        'norm_g': 1.0 + nrm((DEPTH, 2, D), 0.02),
        'ada_w': nrm((DEPTH, D, 6 * D), 0.3 * D ** -0.5),
        'ada_b': nrm((DEPTH, 6 * D), 0.02),
        'mlp_w1': nrm((DEPTH, D, D_FF), D ** -0.5),
        'mlp_w2': nrm((DEPTH, D_FF, D), D_FF ** -0.5),
        'rwkv_mu': jax.random.uniform(next(ks), (N_RWKV, 6, D), jnp.float32),
        'rwkv_w_rkv': nrm((N_RWKV, 3, D, D), D ** -0.5),
        'rwkv_w_o': nrm((N_RWKV, D, D), D ** -0.5),
        'rwkv_w0': nrm((N_RWKV, 2, D), 0.5),
        'rwkv_w1': nrm((N_RWKV, 2, D, DECAY_LORA), D ** -0.5),
        'rwkv_w2': nrm((N_RWKV, 2, DECAY_LORA, D), 0.3 * DECAY_LORA ** -0.5),
        'rwkv_a0': nrm((N_RWKV, 2, D), 0.1),
        'rwkv_a1': nrm((N_RWKV, 2, D, AAA_LORA), D ** -0.5),
        'rwkv_a2': nrm((N_RWKV, 2, AAA_LORA, D), 0.3 * AAA_LORA ** -0.5),
        'rwkv_g1': nrm((N_RWKV, D, GATE_LORA), D ** -0.5),
        'rwkv_g2': nrm((N_RWKV, GATE_LORA, D), GATE_LORA ** -0.5),
        'rwkv_k_k': 0.85 + nrm((N_RWKV, D), 0.02),
        'rwkv_k_a': 1.0 + nrm((N_RWKV, D), 0.02),
        'rwkv_r_k': nrm((N_RWKV, H, K), 0.1),
        'rwkv_ln_w': 1.0 + nrm((N_RWKV, D), 0.02),
        'rwkv_ln_b': nrm((N_RWKV, D), 0.02),
        'na_w_qkv': nrm((N_NA, D, 3 * D), D ** -0.5),
        'na_w_o': nrm((N_NA, D, D), D ** -0.5),
        'na_q_g': 1.0 + nrm((N_NA, NA_HEAD), 0.02),
        'na_k_g': 1.0 + nrm((N_NA, NA_HEAD), 0.02),
        'na_rpb': nrm((N_NA, NA_HEADS, 2 * WIN_ROWS - 1, 2 * WIN_COLS - 1), 0.1),
    }


def reference(x_prompt, x_sample, state_rwkv, cache_na_k, cache_na_v, c, c_ctx,
              norm_g, ada_w, ada_b, mlp_w1, mlp_w2,
              rwkv_mu, rwkv_w_rkv, rwkv_w_o, rwkv_w0, rwkv_w1, rwkv_w2, rwkv_a0, rwkv_a1, rwkv_a2,
              rwkv_g1, rwkv_g2, rwkv_k_k, rwkv_k_a, rwkv_r_k, rwkv_ln_w, rwkv_ln_b,
              na_w_qkv, na_w_o, na_q_g, na_k_g, na_rpb):
    rwkv_params = (rwkv_mu, rwkv_w_rkv, rwkv_w_o, rwkv_w0, rwkv_w1, rwkv_w2, rwkv_a0, rwkv_a1, rwkv_a2,
                   rwkv_g1, rwkv_g2, rwkv_k_k, rwkv_k_a, rwkv_r_k, rwkv_ln_w, rwkv_ln_b)

    xp = x_prompt
    bp = x_prompt.shape[0]
    new_states, new_k, new_v = [], [], []
    for l in range(DEPTH):
        i = l // N_MIXERS
        sh1, sc1, gt1, sh2, sc2, gt2 = ada_mod(c_ctx[None, :], ada_w[l], ada_b[l])
        h = modulate(rms_norm(xp, norm_g[l, 0]), sh1, sc1)
        if l % N_MIXERS == 0:
            s0 = jnp.zeros((bp, 2, RWKV_HEADS, RWKV_HEAD, RWKV_HEAD), xp.dtype)
            o, s_fin = rwkv7_mixer(h, s0, *[p[i] for p in rwkv_params])
            new_states.append(s_fin)
        else:
            o, k_c, v_c = na_context(h, na_w_qkv[i], na_w_o[i], na_q_g[i], na_k_g[i])
            new_k.append(k_c)
            new_v.append(v_c)
        xp = xp + gt1[:, None, :] * o
        h = modulate(rms_norm(xp, norm_g[l, 1]), sh2, sc2)
        xp = xp + gt2[:, None, :] * sq_relu_mlp(h, mlp_w1[l], mlp_w2[l])
    y_prompt = xp

    xs = x_sample
    for l in range(DEPTH):
        i = l // N_MIXERS
        sh1, sc1, gt1, sh2, sc2, gt2 = ada_mod(c, ada_w[l], ada_b[l])
        h = modulate(rms_norm(xs, norm_g[l, 0]), sh1, sc1)
        if l % N_MIXERS == 0:
            o, _ = rwkv7_mixer(h, state_rwkv[:, i], *[p[i] for p in rwkv_params])
        else:
            o = na_latent(h, cache_na_k[:, i], cache_na_v[:, i], na_w_qkv[i], na_w_o[i],
                          na_q_g[i], na_k_g[i], na_rpb[i])
        xs = xs + gt1[:, None, :] * o
        h = modulate(rms_norm(xs, norm_g[l, 1]), sh2, sc2)
        xs = xs + gt2[:, None, :] * sq_relu_mlp(h, mlp_w1[l], mlp_w2[l])
    y_sample = xs

    new_state_rwkv = jnp.stack(new_states, axis=1)
    new_cache_na_k = jnp.stack(new_k, axis=1)
    new_cache_na_v = jnp.stack(new_v, axis=1)
    return (y_prompt, y_sample, new_state_rwkv, new_cache_na_k, new_cache_na_v)
```

```python
import functools

import jax
import jax.numpy as jnp
import numpy as np
from jax import lax
from jax.experimental import pallas as pl
from jax.experimental.pallas import tpu as pltpu

F32 = jnp.float32
BF16 = jnp.bfloat16

D_MODEL = 1024
DEPTH = 4
HEADS = 16
HEAD = 64
LANES = 128
HEADS_PER_TILE = LANES // HEAD
N_HEAD_TILES = HEADS // HEADS_PER_TILE
SUBLANES = 8
D_FF = 4 * D_MODEL
LORA_PAD = 128
GRID_W = 64
WIN_ROWS = 8
WIN_COLS = 16
NORM_EPS = 1e-6
GN_EPS = 64e-5
ATTN_SCALE = HEAD ** -0.5
NEG_BIG = -1e30
DECAY_SCALE = float(np.exp(-0.5))
SCAN_CHUNK = 64
VMEM_LIMIT = 56 * 1024 * 1024

NN = ((1,), (0,))
NT = ((1,), (1,))
TN = ((0,), (0,))


def _dot(a, b, dims=NN):
    return lax.dot_general(a, b, (dims, ((), ())), preferred_element_type=F32)


def _split2(x):
    hi = x.astype(BF16)
    lo = (x - hi.astype(F32)).astype(BF16)
    return hi, lo


def _split3(x):
    hi = x.astype(BF16)
    r1 = x - hi.astype(F32)
    mid = r1.astype(BF16)
    lo = (r1 - mid.astype(F32)).astype(BF16)
    return hi, mid, lo


def _dot3(a, b, dims=NN):
    ah, al = _split2(a)
    bh, bl = _split2(b)
    return _dot(ah, bh, dims) + (_dot(ah, bl, dims) + _dot(al, bh, dims))


def _dot_exact_lhs(a_bf16, b, dims=NN):
    bh, bm, bl = _split3(b)
    return _dot(a_bf16, bh, dims) + (_dot(a_bf16, bm, dims) + _dot(a_bf16, bl, dims))


def _params(*sem):
    return pltpu.CompilerParams(dimension_semantics=sem, vmem_limit_bytes=VMEM_LIMIT)


def _norm_mod(x, g, shift, scale):
    ms = jnp.mean(x * x, axis=-1, keepdims=True)
    return (x * lax.rsqrt(ms + NORM_EPS) * g) * (1.0 + scale) + shift


def _const_spec(shape):
    nd = len(shape)
    return pl.BlockSpec(shape, lambda *_: (0,) * nd)


def _ada_kernel(c_ref, w_ref, b_ref, o_ref):
    c = c_ref[...]
    s = c * jax.nn.sigmoid(c)
    o_ref[0] = _dot3(s, w_ref[0]) + b_ref[0]


def _ada_call(cond, ada_w, ada_b):
    rows = cond.shape[0]
    tn = 768
    n = 6 * D_MODEL
    return pl.pallas_call(
        _ada_kernel,
        grid=(DEPTH, n // tn),
        in_specs=[
            pl.BlockSpec((rows, D_MODEL), lambda l, j: (0, 0)),
            pl.BlockSpec((1, D_MODEL, tn), lambda l, j: (l, 0, j)),
            pl.BlockSpec((1, 1, tn), lambda l, j: (l, 0, j)),
        ],
        out_specs=pl.BlockSpec((1, rows, tn), lambda l, j: (l, 0, j)),
        out_shape=jax.ShapeDtypeStruct((DEPTH, rows, n), F32),
        compiler_params=_params("parallel", "parallel"),
        name="ada_mod",
    )(cond, ada_w, ada_b.reshape(DEPTH, 1, n))


def _rwkv_proj_kernel(x_ref, xp_ref, xn_ref, mod_ref, g_ref, mu_ref, wrkv_ref, wl1_ref, w2p_ref, w0_ref,
                      a2p_ref, a0_ref, g2_ref, r_ref, k_ref, v_ref, gate_ref, lw_ref, a_ref, *, tm, nt):
    i = pl.program_id(1)
    g = g_ref[...]
    shift = mod_ref[0, 0:1, :]
    scale = mod_ref[0, 1:2, :]
    h = _norm_mod(x_ref[0], g, shift, scale)
    hp = _norm_mod(xp_ref[0], g, shift, scale)[SUBLANES - 1:SUBLANES, :]
    hn = _norm_mod(xn_ref[0], g, shift, scale)[0:1, :]
    hp = jnp.where(i == 0, 0.0, hp)
    hn = jnp.where(i == nt - 1, 0.0, hn)
    row = lax.broadcasted_iota(jnp.int32, (tm, 1), 0)
    prev = jnp.where(row == 0, hp, pltpu.roll(h, 1, 0))
    nxt = jnp.where(row == tm - 1, hn, pltpu.roll(h, tm - 1, 0))
    delta = 0.5 * (prev + nxt) - h

    def mix(j):
        return (h + delta * mu_ref[j:j + 1, :]).astype(BF16)

    r_ref[0] = _dot(mix(0), wrkv_ref[0])
    k_ref[0] = _dot(mix(1), wrkv_ref[1])
    v_ref[0] = _dot(mix(2), wrkv_ref[2])
    lw = jnp.tanh(_dot(mix(3), wl1_ref[0])).astype(BF16)
    la = _dot(mix(4), wl1_ref[1]).astype(BF16)
    lg = jax.nn.sigmoid(_dot(mix(5), wl1_ref[2])).astype(BF16)
    gate_ref[0] = _dot(lg, g2_ref[...])
    for d in range(2):
        z = w0_ref[d:d + 1, :] + _dot(lw, w2p_ref[d])
        lw_ref[d, 0] = -DECAY_SCALE * jax.nn.sigmoid(z)
        a_ref[d, 0] = jax.nn.sigmoid(a0_ref[d:d + 1, :] + _dot(la, a2p_ref[d]))


def _rwkv_proj_call(x, mod, g, p, tm):
    B, T, D = x.shape
    nt = T // tm
    tb = tm // SUBLANES
    row_spec = pl.BlockSpec((1, tm, D), lambda b, i: (b, i, 0))
    dir_spec = pl.BlockSpec((2, 1, tm, D), lambda b, i: (0, b, i, 0))
    out_bt = jax.ShapeDtypeStruct((B, T, D), F32)
    out_dir = jax.ShapeDtypeStruct((2, B, T, D), F32)
    return pl.pallas_call(
        functools.partial(_rwkv_proj_kernel, tm=tm, nt=nt),
        grid=(B, nt),
        in_specs=[
            row_spec,
            pl.BlockSpec((1, SUBLANES, D), lambda b, i: (b, jnp.maximum(i * tb - 1, 0), 0)),
            pl.BlockSpec((1, SUBLANES, D), lambda b, i: (b, jnp.minimum((i + 1) * tb, T // SUBLANES - 1), 0)),
            pl.BlockSpec((1, 6, D), lambda b, i: (b, 0, 0)),
            _const_spec((1, D)),
            _const_spec((6, D)),
            _const_spec((3, D, D)),
            _const_spec((3, D, LORA_PAD)),
            _const_spec((2, LORA_PAD, D)),
            _const_spec((2, D)),
            _const_spec((2, LORA_PAD, D)),
            _const_spec((2, D)),
            _const_spec((LORA_PAD, D)),
        ],
        out_specs=[row_spec, row_spec, row_spec, row_spec, dir_spec, dir_spec],
        out_shape=[out_bt, out_bt, out_bt, out_bt, out_dir, out_dir],
        compiler_params=_params("parallel", "parallel"),
        name="rwkv_proj",
    )(x, x, x, mod, g, p["mu"], p["w_rkv"], p["wl1"], p["w2p"], p["w0"], p["a2p"], p["a0"], p["g2"])


def _scan_kernel(r_ref, k_ref, v_ref, lw_ref, a_ref, s0_ref, kk_ref, ka_ref, rk_ref,
                 y_ref, bon_ref, sf_ref, st_ref, *, L, nc):
    d = pl.program_id(1)
    c = pl.program_id(3)

    @pl.when(c == 0)
    def _():
        st_ref[...] = s0_ref[0, 0]

    row = lax.broadcasted_iota(jnp.int32, (L, L), 0)
    col = lax.broadcasted_iota(jnp.int32, (L, L), 1)
    order = (row - col) * (1 - 2 * d)
    strict = order > 0
    incl = order >= 0
    tri = jnp.where(incl, 1.0, 0.0).astype(BF16)
    hrow = lax.broadcasted_iota(jnp.int32, (HEAD, HEAD), 0)
    hcol = lax.broadcasted_iota(jnp.int32, (HEAD, HEAD), 1)
    eye = hrow == hcol

    r2, k2, v2 = r_ref[0], k_ref[0], v_ref[0]
    lw2, a2 = lw_ref[0, 0], a_ref[0, 0]
    for hh in range(HEADS_PER_TILE):
        sl = slice(hh * HEAD, (hh + 1) * HEAD)
        r, k, v, lw, a = r2[:, sl], k2[:, sl], v2[:, sl], lw2[:, sl], a2[:, sl]
        kkr = k * kk_ref[:, sl]
        kk = kkr * lax.rsqrt(jnp.sum(kkr * kkr, axis=-1, keepdims=True) + 1e-12)
        kd = k * (1.0 + (a - 1.0) * ka_ref[:, sl])
        b = kk * a
        bon_ref[0, 0, :, sl] = jnp.sum(r * kd * rk_ref[:, sl], axis=-1, keepdims=True) * v

        cum = _dot_exact_lhs(tri, lw)
        tot = jnp.sum(lw, axis=0, keepdims=True)
        al = -kk * jnp.exp(cum - lw)
        rh = r * jnp.exp(cum)
        e_neg = jnp.exp(-cum)
        bc = b * e_neg
        kc = kd * e_neg
        e_tail = jnp.exp(tot - cum)

        m_ab = jnp.where(strict, _dot3(al, bc, NT), 0.0)
        m_ak = jnp.where(strict, _dot3(al, kc, NT), 0.0)
        n_rb = jnp.where(incl, _dot3(rh, bc, NT), 0.0)
        n_rk = jnp.where(incl, _dot3(rh, kc, NT), 0.0)
        st = st_ref[hh]
        u = _dot3(al, st) + _dot3(m_ak, v)
        pw = m_ab
        span = 1
        while span < L:
            u = u + _dot3(pw, u)
            span *= 2
            if span < L:
                pw = _dot3(pw, pw)
        y_ref[0, 0, :, sl] = _dot3(rh, st) + _dot3(n_rb, u) + _dot3(n_rk, v)
        tot_col = jnp.sum(jnp.where(eye, jnp.broadcast_to(jnp.exp(tot), (HEAD, HEAD)), 0.0), axis=1, keepdims=True)
        st_ref[hh] = tot_col * st + _dot3(b * e_tail, u, TN) + _dot3(kd * e_tail, v, TN)

    @pl.when(c == nc - 1)
    def _():
        sf_ref[0, 0] = st_ref[...]


def _rwkv_scan_call(r, k, v, lw, a, s0, p):
    B, T, D = r.shape
    L = SCAN_CHUNK
    nc = T // L

    def cidx(d, c):
        return c + d * (nc - 1 - 2 * c)

    row_spec = pl.BlockSpec((1, L, LANES), lambda b, d, h, c: (b, cidx(d, c), h))
    dir_spec = pl.BlockSpec((1, 1, L, LANES), lambda b, d, h, c: (d, b, cidx(d, c), h))
    st_spec = pl.BlockSpec((1, 1, HEADS_PER_TILE, HEAD, HEAD), lambda b, d, h, c: (b, d, h, 0, 0))
    vec_spec = pl.BlockSpec((1, LANES), lambda b, d, h, c: (0, h))
    return pl.pallas_call(
        functools.partial(_scan_kernel, L=L, nc=nc),
        grid=(B, 2, N_HEAD_TILES, nc),
        in_specs=[row_spec, row_spec, row_spec, dir_spec, dir_spec, st_spec, vec_spec, vec_spec, vec_spec],
        out_specs=[dir_spec, dir_spec, st_spec],
        out_shape=[jax.ShapeDtypeStruct((2, B, T, D), F32), jax.ShapeDtypeStruct((2, B, T, D), F32),
                   jax.ShapeDtypeStruct((B, 2, HEADS, HEAD, HEAD), F32)],
        scratch_shapes=[pltpu.VMEM((HEADS_PER_TILE, HEAD, HEAD), F32)],
        compiler_params=_params("parallel", "parallel", "parallel", "arbitrary"),
        name="rwkv_scan",
    )(r, k, v, lw, a, s0, p["k_k"], p["k_a"], p["r_k"])


def _seg_mean(x, e_ref, et_ref):
    xh, xl = _split2(x)
    s = (_dot(xh, e_ref[...]) + _dot(xl, e_ref[...])) * (1.0 / HEAD)
    sh, slo = _split2(s)
    return _dot(sh, et_ref[...]) + _dot(slo, et_ref[...])


def _rwkv_post_kernel(y_ref, bon_ref, gate_ref, x_ref, mod_ref, lnw_ref, lnb_ref, e_ref, et_ref, wo_ref, o_ref):
    y = y_ref[0, 0] + y_ref[1, 0]
    yc = y - _seg_mean(y, e_ref, et_ref)
    var = _seg_mean(yc * yc, e_ref, et_ref)
    yn = yc * lax.rsqrt(var + GN_EPS) * lnw_ref[...] + lnb_ref[...]
    o = (yn + (bon_ref[0, 0] + bon_ref[1, 0])) * gate_ref[0]
    o_ref[0] = x_ref[0] + mod_ref[0, 2:3, :] * _dot(o.astype(BF16), wo_ref[...])


def _rwkv_post_call(y, bon, gate, x, mod, p, seg, tm):
    B, T, D = x.shape
    row_spec = pl.BlockSpec((1, tm, D), lambda b, i: (b, i, 0))
    dir_spec = pl.BlockSpec((2, 1, tm, D), lambda b, i: (0, b, i, 0))
    return pl.pallas_call(
        _rwkv_post_kernel,
        grid=(B, T // tm),
        in_specs=[dir_spec, dir_spec, row_spec, row_spec, pl.BlockSpec((1, 6, D), lambda b, i: (b, 0, 0)),
                  _const_spec((1, D)), _const_spec((1, D)), _const_spec((D, LANES)), _const_spec((LANES, D)),
                  _const_spec((D, D))],
        out_specs=row_spec,
        out_shape=jax.ShapeDtypeStruct((B, T, D), F32),
        compiler_params=_params("parallel", "parallel"),
        name="rwkv_post",
    )(y, bon, gate, x, mod, p["ln_w"], p["ln_b"], seg[0], seg[1], p["w_o"])


def _mlp_kernel(x_ref, mod_ref, g_ref, w1_ref, w2_ref, o_ref, *, nchunk):
    x = x_ref[0]
    h = _norm_mod(x, g_ref[...], mod_ref[0, 3:4, :], mod_ref[0, 4:5, :]).astype(BF16)
    fc = D_FF // nchunk
    acc = jnp.zeros(x.shape, F32)
    for j in range(nchunk):
        hid = jnp.maximum(_dot(h, w1_ref[:, j * fc:(j + 1) * fc]), 0.0)
        acc = acc + _dot((hid * hid).astype(BF16), w2_ref[j * fc:(j + 1) * fc, :])
    o_ref[0] = x + mod_ref[0, 5:6, :] * acc


def _mlp_call(x, mod, g, w1, w2, tm):
    B, T, D = x.shape
    row_spec = pl.BlockSpec((1, tm, D), lambda b, i: (b, i, 0))
    return pl.pallas_call(
        functools.partial(_mlp_kernel, nchunk=4),
        grid=(B, T // tm),
        in_specs=[row_spec, pl.BlockSpec((1, 6, D), lambda b, i: (b, 0, 0)), _const_spec((1, D)),
                  _const_spec((D, D_FF)), _const_spec((D_FF, D))],
        out_specs=row_spec,
        out_shape=jax.ShapeDtypeStruct((B, T, D), F32),
        compiler_params=_params("parallel", "parallel"),
        name="mlp",
    )(x, mod, g, w1, w2)


def _na_qkv_kernel(x_ref, mod_ref, g_ref, w_ref, o_ref):
    h = _norm_mod(x_ref[0], g_ref[...], mod_ref[0, 0:1, :], mod_ref[0, 1:2, :]).astype(BF16)
    o_ref[0] = _dot(h, w_ref[...])


def _na_qkv_call(x, mod, g, w, tm):
    B, T, D = x.shape
    return pl.pallas_call(
        _na_qkv_kernel,
        grid=(B, T // tm),
        in_specs=[pl.BlockSpec((1, tm, D), lambda b, i: (b, i, 0)), pl.BlockSpec((1, 6, D), lambda b, i: (b, 0, 0)),
                  _const_spec((1, D)), _const_spec((D, 3 * D))],
        out_specs=pl.BlockSpec((1, tm, 3 * D), lambda b, i: (b, i, 0)),
        out_shape=jax.ShapeDtypeStruct((B, T, 3 * D), F32),
        compiler_params=_params("parallel", "parallel"),
        name="na_qkv",
    )(x, mod, g, w)


def _head_norm(x, g):
    return x * lax.rsqrt(jnp.mean(x * x, axis=-1, keepdims=True) + NORM_EPS) * g


def _na_ctx_kernel(q_ref, k_ref, v_ref, qg_ref, kg_ref, o_ref, kc_ref, vc_ref):
    q2, k2, v2 = q_ref[0], k_ref[0], v_ref[0]
    for hh in range(HEADS_PER_TILE):
        sl = slice(hh * HEAD, (hh + 1) * HEAD)
        q = _head_norm(q2[:, sl], qg_ref[...])
        k = _head_norm(k2[:, sl], kg_ref[...])
        v = v2[:, sl]
        kc_ref[0, hh] = k
        vc_ref[0, hh] = v
        s = _dot(q.astype(BF16), k.astype(BF16), NT) * ATTN_SCALE
        e = jnp.exp(s - jnp.max(s, axis=-1, keepdims=True))
        o = _dot(e.astype(BF16), v.astype(BF16)) / jnp.sum(e, axis=-1, keepdims=True)
        o_ref[0, :, sl] = o


def _qkv_specs(T):
    return [pl.BlockSpec((1, T, LANES), lambda b, h, part=part: (b, 0, part * N_HEAD_TILES + h)) for part in range(3)]


def _na_ctx_call(qkv, qg, kg):
    B, T, _ = qkv.shape
    cache_spec = pl.BlockSpec((1, HEADS_PER_TILE, T, HEAD), lambda b, h: (b, h, 0, 0))
    cache_shape = jax.ShapeDtypeStruct((B, HEADS, T, HEAD), F32)
    return pl.pallas_call(
        _na_ctx_kernel,
        grid=(B, N_HEAD_TILES),
        in_specs=_qkv_specs(T) + [_const_spec((1, HEAD)), _const_spec((1, HEAD))],
        out_specs=[pl.BlockSpec((1, T, LANES), lambda b, h: (b, 0, h)), cache_spec, cache_spec],
        out_shape=[jax.ShapeDtypeStruct((B, T, D_MODEL), F32), cache_shape, cache_shape],
        compiler_params=_params("parallel", "parallel"),
        name="na_ctx",
    )(qkv, qkv, qkv, qg, kg)


def _na_lat_kernel(q_ref, k_ref, v_ref, kc_ref, vc_ref, bias_ref, qg_ref, kg_ref, o_ref, qn_ref, kn_ref, vn_ref,
                   *, rows):
    q2, k2, v2 = q_ref[0], k_ref[0], v_ref[0]
    for hh in range(HEADS_PER_TILE):
        sl = slice(hh * HEAD, (hh + 1) * HEAD)
        qn_ref[hh] = _head_norm(q2[:, sl], qg_ref[...]).astype(BF16)
        kn_ref[hh] = _head_norm(k2[:, sl], kg_ref[...]).astype(BF16)
        vn_ref[hh] = v2[:, sl].astype(BF16)
    win = WIN_ROWS * GRID_W

    def row_block(r, carry):
        rs = jnp.clip(r - WIN_ROWS // 2, 0, rows - WIN_ROWS)
        var = rs - r + WIN_ROWS - 1
        q0 = pl.multiple_of(r * GRID_W, GRID_W)
        k0 = pl.multiple_of(rs * GRID_W, GRID_W)
        for hh in range(HEADS_PER_TILE):
            q = qn_ref[hh, pl.ds(q0, GRID_W), :]
            kw = kn_ref[hh, pl.ds(k0, win), :]
            vw = vn_ref[hh, pl.ds(k0, win), :]
            s_w = _dot(q, kw, NT) * ATTN_SCALE + bias_ref[hh, var]
            s_c = _dot(q, kc_ref[0, hh].astype(BF16), NT) * ATTN_SCALE
            m = jnp.maximum(jnp.max(s_w, axis=-1, keepdims=True), jnp.max(s_c, axis=-1, keepdims=True))
            e_w = jnp.exp(s_w - m)
            e_c = jnp.exp(s_c - m)
            den = jnp.sum(e_w, axis=-1, keepdims=True) + jnp.sum(e_c, axis=-1, keepdims=True)
            o = _dot(e_w.astype(BF16), vw) + _dot(e_c.astype(BF16), vc_ref[0, hh].astype(BF16))
            o_ref[0, pl.ds(q0, GRID_W), hh * HEAD:(hh + 1) * HEAD] = o / den
        return carry

    lax.fori_loop(0, rows, row_block, 0)


def _na_lat_call(qkv, k_ctx, v_ctx, bias, qg, kg):
    B, T, _ = qkv.shape
    P = k_ctx.shape[2]
    rows = T // GRID_W
    ctx_spec = pl.BlockSpec((1, HEADS_PER_TILE, P, HEAD), lambda b, h: (b, h, 0, 0))
    return pl.pallas_call(
        functools.partial(_na_lat_kernel, rows=rows),
        grid=(B, N_HEAD_TILES),
        in_specs=_qkv_specs(T) + [
            ctx_spec, ctx_spec,
            pl.BlockSpec((HEADS_PER_TILE, WIN_ROWS, GRID_W, WIN_ROWS * GRID_W), lambda b, h: (h, 0, 0, 0)),
            _const_spec((1, HEAD)), _const_spec((1, HEAD))],
        out_specs=pl.BlockSpec((1, T, LANES), lambda b, h: (b, 0, h)),
        out_shape=jax.ShapeDtypeStruct((B, T, D_MODEL), F32),
        scratch_shapes=[pltpu.VMEM((HEADS_PER_TILE, T, HEAD), BF16)] * 3,
        compiler_params=_params("parallel", "parallel"),
        name="na_lat",
    )(qkv, qkv, qkv, k_ctx, v_ctx, bias, qg, kg)


def _na_bias_table(rpb):
    align = np.arange(WIN_ROWS)[:, None] + np.arange(WIN_ROWS)[None, :]
    qc = np.arange(GRID_W)[:, None]
    kc = np.arange(GRID_W)[None, :]
    col = np.clip(kc - qc + WIN_COLS - 1, 0, 2 * WIN_COLS - 2)
    ws = np.clip(qc - WIN_COLS // 2, 0, GRID_W - WIN_COLS)
    valid = (kc >= ws) & (kc < ws + WIN_COLS)
    tab = rpb[:, align[:, :, None, None], col[None, None, :, :]]
    tab = jnp.where(valid[None, None, None], tab, NEG_BIG)
    return tab.transpose(0, 1, 3, 2, 4).reshape(HEADS, WIN_ROWS, GRID_W, WIN_ROWS * GRID_W)


def _out_proj_kernel(a_ref, x_ref, mod_ref, w_ref, o_ref):
    o_ref[0] = x_ref[0] + mod_ref[0, 2:3, :] * _dot(a_ref[0].astype(BF16), w_ref[...])


def _out_proj_call(a, x, mod, w, tm):
    B, T, D = x.shape
    row_spec = pl.BlockSpec((1, tm, D), lambda b, i: (b, i, 0))
    return pl.pallas_call(
        _out_proj_kernel,
        grid=(B, T // tm),
        in_specs=[row_spec, row_spec, pl.BlockSpec((1, 6, D), lambda b, i: (b, 0, 0)), _const_spec((D, D))],
        out_specs=row_spec,
        out_shape=jax.ShapeDtypeStruct((B, T, D), F32),
        compiler_params=_params("parallel", "parallel"),
        name="out_proj",
    )(a, x, mod, w)


def _pad_lora_in(w):
    return jnp.pad(w, ((0, 0), (0, LORA_PAD - w.shape[1])))


def _rwkv_layer_params(i, mu, w_rkv, w_o, w0, w1, w2, a0, a1, a2, g1, g2, k_k, k_a, r_k, ln_w, ln_b):
    rank = w1.shape[-1]
    wl1 = jnp.stack([jnp.concatenate([w1[i, 0], w1[i, 1]], axis=1),
                     jnp.concatenate([a1[i, 0], a1[i, 1]], axis=1),
                     _pad_lora_in(g1[i])]).astype(BF16)
    zeros = jnp.zeros((rank, D_MODEL), F32)
    w2p = jnp.stack([jnp.concatenate([w2[i, 0], zeros]), jnp.concatenate([zeros, w2[i, 1]])]).astype(BF16)
    a2p = jnp.stack([jnp.concatenate([a2[i, 0], zeros]), jnp.concatenate([zeros, a2[i, 1]])]).astype(BF16)
    g2p = jnp.pad(g2[i], ((0, LORA_PAD - g2.shape[1]), (0, 0))).astype(BF16)
    return dict(mu=mu[i], w_rkv=w_rkv[i].astype(BF16), wl1=wl1, w2p=w2p, w0=w0[i], a2p=a2p, a0=a0[i], g2=g2p,
                k_k=k_k[i][None], k_a=k_a[i][None], r_k=r_k[i].reshape(1, D_MODEL),
                ln_w=ln_w[i][None], ln_b=ln_b[i][None], w_o=w_o[i].astype(BF16))


def _segment_matrices():
    e = (np.arange(D_MODEL)[:, None] // HEAD == np.arange(LANES)[None, :]).astype(np.float32)
    return jnp.asarray(e, BF16), jnp.asarray(e.T, BF16)


def kernel(x_prompt, x_sample, state_rwkv, cache_na_k, cache_na_v, c, c_ctx, norm_g, ada_w, ada_b, mlp_w1, mlp_w2, rwkv_mu, rwkv_w_rkv, rwkv_w_o, rwkv_w0, rwkv_w1, rwkv_w2, rwkv_a0, rwkv_a1, rwkv_a2, rwkv_g1, rwkv_g2, rwkv_k_k, rwkv_k_a, rwkv_r_k, rwkv_ln_w, rwkv_ln_b, na_w_qkv, na_w_o, na_q_g, na_k_g, na_rpb):
    n_dec = c.shape[0]
    bp = x_prompt.shape[0]
    cond_rows = 16
    cond = jnp.zeros((cond_rows, D_MODEL), F32).at[:n_dec].set(c).at[n_dec].set(c_ctx)
    mods = _ada_call(cond, ada_w, ada_b)
    mod_lat = mods[:, :n_dec].reshape(DEPTH, n_dec, 6, D_MODEL)
    mod_ctx = jnp.broadcast_to(mods[:, n_dec].reshape(DEPTH, 1, 6, D_MODEL), (DEPTH, bp, 6, D_MODEL))

    rwkv_raw = (rwkv_mu, rwkv_w_rkv, rwkv_w_o, rwkv_w0, rwkv_w1, rwkv_w2, rwkv_a0, rwkv_a1, rwkv_a2,
                rwkv_g1, rwkv_g2, rwkv_k_k, rwkv_k_a, rwkv_r_k, rwkv_ln_w, rwkv_ln_b)
    n_rwkv = rwkv_mu.shape[0]
    n_na = na_w_qkv.shape[0]
    rwkv_p = [_rwkv_layer_params(i, *rwkv_raw) for i in range(n_rwkv)]
    seg = _segment_matrices()
    w1_bf = mlp_w1.astype(BF16)
    w2_bf = mlp_w2.astype(BF16)
    wqkv_bf = na_w_qkv.astype(BF16)
    wo_bf = na_w_o.astype(BF16)
    bias_tabs = [_na_bias_table(na_rpb[i]) for i in range(n_na)]

    def run(x, mod_all, tm, s0_fn, attn_fn):
        states = []
        for l in range(DEPTH):
            i = l // 2
            mod = mod_all[l]
            if l % 2 == 0:
                p = rwkv_p[i]
                r, k, v, gate, lw, a = _rwkv_proj_call(x, mod, norm_g[l, 0][None], p, tm)
                y, bon, s_fin = _rwkv_scan_call(r, k, v, lw, a, s0_fn(i), p)
                states.append(jnp.swapaxes(s_fin, -1, -2))
                x = _rwkv_post_call(y, bon, gate, x, mod, p, seg, tm)
            else:
                qkv = _na_qkv_call(x, mod, norm_g[l, 0][None], wqkv_bf[i], tm)
                o = attn_fn(i, qkv)
                x = _out_proj_call(o, x, mod, wo_bf[i], tm)
            x = _mlp_call(x, mod, norm_g[l, 1][None], w1_bf[l], w2_bf[l], tm)
        return x, states

    new_k, new_v = [], []

    def ctx_attn(i, qkv):
        o, k_c, v_c = _na_ctx_call(qkv, na_q_g[i][None], na_k_g[i][None])
        new_k.append(k_c)
        new_v.append(v_c)
        return o

    def lat_attn(i, qkv):
        return _na_lat_call(qkv, cache_na_k[:, i], cache_na_v[:, i], bias_tabs[i], na_q_g[i][None], na_k_g[i][None])

    zero_state = jnp.zeros((bp, 2, HEADS, HEAD, HEAD), F32)
    y_prompt, new_states = run(x_prompt, mod_ctx, 256, lambda i: zero_state, ctx_attn)
    y_sample, _ = run(x_sample, mod_lat, 256, lambda i: jnp.swapaxes(state_rwkv[:, i], -1, -2), lat_attn)
    return (y_prompt, y_sample, jnp.stack(new_states, axis=1), jnp.stack(new_k, axis=1), jnp.stack(new_v, axis=1))
```

```python
import functools

import jax
import jax.numpy as jnp
import numpy as np
from jax import lax
from jax.experimental import pallas as pl
from jax.experimental.pallas import tpu as pltpu

F32 = jnp.float32
BF16 = jnp.bfloat16

D_MODEL = 1024
DEPTH = 4
HEADS = 16
HEAD = 64
LANES = 128
HEADS_PER_TILE = LANES // HEAD
N_HEAD_TILES = HEADS // HEADS_PER_TILE
SUBLANES = 8
D_FF = 4 * D_MODEL
LORA_PAD = 128
GRID_W = 64
WIN_ROWS = 8
WIN_COLS = 16
NORM_EPS = 1e-6
GN_EPS = 64e-5
ATTN_SCALE = HEAD ** -0.5
NEG_BIG = -1e30
DECAY_SCALE = float(np.exp(-0.5))
SCAN_CHUNK = 64
VMEM_LIMIT = 56 * 1024 * 1024

NN = ((1,), (0,))
NT = ((1,), (1,))
TN = ((0,), (0,))


def _dot(a, b, dims=NN):
    return lax.dot_general(a, b, (dims, ((), ())), preferred_element_type=F32)


def _split2(x):
    hi = x.astype(BF16)
    lo = (x - hi.astype(F32)).astype(BF16)
    return hi, lo


def _split3(x):
    hi = x.astype(BF16)
    r1 = x - hi.astype(F32)
    mid = r1.astype(BF16)
    lo = (r1 - mid.astype(F32)).astype(BF16)
    return hi, mid, lo


def _dot3(a, b, dims=NN):
    ah, al = _split2(a)
    bh, bl = _split2(b)
    return _dot(ah, bh, dims) + (_dot(ah, bl, dims) + _dot(al, bh, dims))


def _dot_exact_lhs(a_bf16, b, dims=NN):
    bh, bm, bl = _split3(b)
    return _dot(a_bf16, bh, dims) + (_dot(a_bf16, bm, dims) + _dot(a_bf16, bl, dims))


def _params(*sem):
    return pltpu.CompilerParams(dimension_semantics=sem, vmem_limit_bytes=VMEM_LIMIT)


def _norm_mod(x, g, shift, scale):
    ms = jnp.mean(x * x, axis=-1, keepdims=True)
    return (x * lax.rsqrt(ms + NORM_EPS) * g) * (1.0 + scale) + shift


def _const_spec(shape):
    nd = len(shape)
    return pl.BlockSpec(shape, lambda *_: (0,) * nd)


def _ada_kernel(c_ref, w_ref, b_ref, o_ref):
    c = c_ref[...]
    s = c * jax.nn.sigmoid(c)
    o_ref[0] = _dot3(s, w_ref[0]) + b_ref[0]


def _ada_call(cond, ada_w, ada_b):
    rows = cond.shape[0]
    tn = 768
    n = 6 * D_MODEL
    return pl.pallas_call(
        _ada_kernel,
        grid=(DEPTH, n // tn),
        in_specs=[
            pl.BlockSpec((rows, D_MODEL), lambda l, j: (0, 0)),
            pl.BlockSpec((1, D_MODEL, tn), lambda l, j: (l, 0, j)),
            pl.BlockSpec((1, 1, tn), lambda l, j: (l, 0, j)),
        ],
        out_specs=pl.BlockSpec((1, rows, tn), lambda l, j: (l, 0, j)),
        out_shape=jax.ShapeDtypeStruct((DEPTH, rows, n), F32),
        compiler_params=_params("parallel", "parallel"),
        name="ada_mod",
    )(cond, ada_w, ada_b.reshape(DEPTH, 1, n))


def _rwkv_proj_kernel(x_ref, xp_ref, xn_ref, mod_ref, g_ref, mu_ref, wrkv_ref, wl1_ref, w2p_ref, w0_ref,
                      a2p_ref, a0_ref, g2_ref, r_ref, k_ref, v_ref, gate_ref, lw_ref, a_ref, *, tm, nt):
    i = pl.program_id(1)
    g = g_ref[...]
    shift = mod_ref[0, 0:1, :]
    scale = mod_ref[0, 1:2, :]
    h = _norm_mod(x_ref[0], g, shift, scale)
    hp = _norm_mod(xp_ref[0], g, shift, scale)[SUBLANES - 1:SUBLANES, :]
    hn = _norm_mod(xn_ref[0], g, shift, scale)[0:1, :]
    hp = jnp.where(i == 0, 0.0, hp)
    hn = jnp.where(i == nt - 1, 0.0, hn)
    row = lax.broadcasted_iota(jnp.int32, (tm, 1), 0)
    prev = jnp.where(row == 0, hp, pltpu.roll(h, 1, 0))
    nxt = jnp.where(row == tm - 1, hn, pltpu.roll(h, tm - 1, 0))
    delta = 0.5 * (prev + nxt) - h

    def mix(j):
        return (h + delta * mu_ref[j:j + 1, :]).astype(BF16)

    r_ref[0] = _dot(mix(0), wrkv_ref[0])
    k_ref[0] = _dot(mix(1), wrkv_ref[1])
    v_ref[0] = _dot(mix(2), wrkv_ref[2])
    lw = jnp.tanh(_dot(mix(3), wl1_ref[0])).astype(BF16)
    la = _dot(mix(4), wl1_ref[1]).astype(BF16)
    lg = jax.nn.sigmoid(_dot(mix(5), wl1_ref[2])).astype(BF16)
    gate_ref[0] = _dot(lg, g2_ref[...])
    for d in range(2):
        z = w0_ref[d:d + 1, :] + _dot(lw, w2p_ref[d])
        lw_ref[d, 0] = -DECAY_SCALE * jax.nn.sigmoid(z)
        a_ref[d, 0] = jax.nn.sigmoid(a0_ref[d:d + 1, :] + _dot(la, a2p_ref[d]))


def _rwkv_proj_call(x, mod, g, p, tm):
    B, T, D = x.shape
    nt = T // tm
    tb = tm // SUBLANES
    row_spec = pl.BlockSpec((1, tm, D), lambda b, i: (b, i, 0))
    dir_spec = pl.BlockSpec((2, 1, tm, D), lambda b, i: (0, b, i, 0))
    out_bt = jax.ShapeDtypeStruct((B, T, D), F32)
    out_dir = jax.ShapeDtypeStruct((2, B, T, D), F32)
    return pl.pallas_call(
        functools.partial(_rwkv_proj_kernel, tm=tm, nt=nt),
        grid=(B, nt),
        in_specs=[
            row_spec,
            pl.BlockSpec((1, SUBLANES, D), lambda b, i: (b, jnp.maximum(i * tb - 1, 0), 0)),
            pl.BlockSpec((1, SUBLANES, D), lambda b, i: (b, jnp.minimum((i + 1) * tb, T // SUBLANES - 1), 0)),
            pl.BlockSpec((1, 6, D), lambda b, i: (b, 0, 0)),
            _const_spec((1, D)),
            _const_spec((6, D)),
            _const_spec((3, D, D)),
            _const_spec((3, D, LORA_PAD)),
            _const_spec((2, LORA_PAD, D)),
            _const_spec((2, D)),
            _const_spec((2, LORA_PAD, D)),
            _const_spec((2, D)),
            _const_spec((LORA_PAD, D)),
        ],
        out_specs=[row_spec, row_spec, row_spec, row_spec, dir_spec, dir_spec],
        out_shape=[out_bt, out_bt, out_bt, out_bt, out_dir, out_dir],
        compiler_params=_params("parallel", "parallel"),
        name="rwkv_proj",
    )(x, x, x, mod, g, p["mu"], p["w_rkv"], p["wl1"], p["w2p"], p["w0"], p["a2p"], p["a0"], p["g2"])


def _scan_kernel(r_ref, k_ref, v_ref, lw_ref, a_ref, s0_ref, kk_ref, ka_ref, rk_ref,
                 y_ref, bon_ref, sf_ref, s_ref, qt_ref, y0_ref, gh_ref, et_ref, *, L, n, nb, hpb):
    d = pl.program_id(1)
    blk = pl.program_id(3)
    TB = n * L

    @pl.when(blk == 0)
    def _():
        s_ref[...] = s0_ref[0, 0]

    sign = 1 - 2 * d
    row = lax.broadcasted_iota(jnp.int32, (TB, TB), 0)
    col = lax.broadcasted_iota(jnp.int32, (TB, TB), 1)
    same = (row // L) == (col // L)
    tri = jnp.where(same & ((row - col) * sign >= 0), 1.0, 0.0).astype(BF16)
    ones_blk = jnp.where(same, 1.0, 0.0).astype(BF16)
    lrow = lax.broadcasted_iota(jnp.int32, (LANES, LANES), 0)
    lcol = lax.broadcasted_iota(jnp.int32, (LANES, LANES), 1)
    head_ones = jnp.where((lrow // HEAD) == (lcol // HEAD), 1.0, 0.0).astype(BF16)
    gorder = ((lrow % L) - (lcol % L)) * sign
    gmask = gorder + jnp.where(lrow < L, 0, 1) > 0
    eye = jnp.where(lax.broadcasted_iota(jnp.int32, (L, L), 0) == lax.broadcasted_iota(jnp.int32, (L, L), 1), 1.0, 0.0)

    def seg_sum(x):
        xh, xl = _split2(x)
        tiles = [slice(t * LANES, (t + 1) * LANES) for t in range(hpb // HEADS_PER_TILE)]
        return jnp.concatenate([_dot(xh[:, t], head_ones) + _dot(xl[:, t], head_ones) for t in tiles], axis=1)

    r2, k2, v2 = r_ref[0], k_ref[0], v_ref[0]
    lw2, a2 = lw_ref[0, 0], a_ref[0, 0]
    kkr = k2 * kk_ref[...]
    kk = kkr * lax.rsqrt(seg_sum(kkr * kkr) + 1e-12)
    kd = k2 * (1.0 + (a2 - 1.0) * ka_ref[...])
    b2 = kk * a2
    bon_ref[0, 0] = seg_sum(r2 * kd * rk_ref[...]) * v2

    cum = _dot_exact_lhs(tri, lw2)
    tot = _dot_exact_lhs(ones_blk, lw2)
    al2 = -kk * jnp.exp(cum - lw2)
    rh2 = r2 * jnp.exp(cum)
    e_neg = jnp.exp(-cum)
    bc2 = b2 * e_neg
    kc2 = kd * e_neg
    e_tail = jnp.exp(tot - cum)
    be2 = b2 * e_tail
    ke2 = kd * e_tail
    et_ref[...] = jnp.exp(tot)

    chains = [(j, hh) for j in range(n) for hh in range(hpb)]
    nch = range(len(chains))

    def part(x, i):
        j, hh = chains[i]
        return x[j * L:(j + 1) * L, hh * HEAD:(hh + 1) * HEAD]

    zeros_h = jnp.zeros((L, HEAD), F32)
    al = [part(al2, i) for i in nch]
    rh = [part(rh2, i) for i in nch]
    v = [part(v2, i) for i in nch]
    lhs = [jnp.concatenate([al[i], rh[i]], axis=0).astype(BF16) for i in nch]
    rhs = [jnp.concatenate([part(bc2, i), part(kc2, i)], axis=0).astype(BF16) for i in nch]
    bek = [jnp.concatenate([part(be2, i), part(ke2, i)], axis=0).astype(BF16) for i in nch]
    g = [jnp.where(gmask, _dot(lhs[i], rhs[i], NT), 0.0) for i in nch]
    gb = [x.astype(BF16) for x in g]
    w = [_dot(gb[i][:L], jnp.concatenate([zeros_h, v[i]], axis=0).astype(BF16)) for i in nch]
    pw = [gb[i][:L, :L] for i in nch]
    t = [eye + g[i][:L, :L] for i in nch]
    span = 2
    while span < L:
        pw = [_dot(pw[i], pw[i]).astype(BF16) for i in nch]
        t = [t[i] + _dot(t[i].astype(BF16), pw[i]) for i in nch]
        span *= 2
    tx = [_dot(t[i].astype(BF16), jnp.concatenate([al[i], w[i]], axis=1).astype(BF16)) for i in nch]
    z = [jnp.concatenate([tx[i], jnp.concatenate([zeros_h, v[i]], axis=1)], axis=0).astype(BF16) for i in nch]
    qy = [_dot(gb[i][L:], z[i]) for i in nch]
    gh = [_dot(z[i], bek[i], TN) for i in nch]
    for i in nch:
        j, hh = chains[i]
        qt_ref[hh, j * L:(j + 1) * L, :] = rh[i] + qy[i][:, :HEAD]
        y0_ref[hh, j * L:(j + 1) * L, :] = qy[i][:, HEAD:]
        gh_ref[hh, j * LANES:(j + 1) * LANES, :] = gh[i]

    def chunk_step(jj, carry):
        cj = jj + d * (n - 1 - 2 * jj)
        r0 = pl.multiple_of(cj * L, L)
        g0 = pl.multiple_of(cj * LANES, LANES)
        s = [s_ref[hh] for hh in range(hpb)]
        sb = [x.astype(BF16) for x in s]
        y = [_dot(qt_ref[hh, pl.ds(r0, L), :].astype(BF16), sb[hh], NT) for hh in range(hpb)]
        sg = [_dot(sb[hh], gh_ref[hh, pl.ds(g0, HEAD), :].astype(BF16)) for hh in range(hpb)]
        for hh in range(hpb):
            y_ref[0, 0, pl.ds(r0, L), hh * HEAD:(hh + 1) * HEAD] = y[hh] + y0_ref[hh, pl.ds(r0, L), :]
            decay = et_ref[pl.ds(r0, SUBLANES), hh * HEAD:(hh + 1) * HEAD][0:1, :]
            s_ref[hh] = s[hh] * decay + sg[hh] + gh_ref[hh, pl.ds(g0 + HEAD, HEAD), :]
        return carry

    lax.fori_loop(0, n, chunk_step, 0, unroll=True)

    @pl.when(blk == nb - 1)
    def _():
        sf_ref[0, 0] = s_ref[...]


def _rwkv_scan_call(r, k, v, lw, a, s0, p, tb, hpb):
    B, T, D = r.shape
    L = SCAN_CHUNK
    n = tb // L
    nb = T // tb
    wl = hpb * HEAD

    def bidx(d, c):
        return c + d * (nb - 1 - 2 * c)

    row_spec = pl.BlockSpec((1, tb, wl), lambda b, d, h, c: (b, bidx(d, c), h))
    dir_spec = pl.BlockSpec((1, 1, tb, wl), lambda b, d, h, c: (d, b, bidx(d, c), h))
    st_spec = pl.BlockSpec((1, 1, hpb, HEAD, HEAD), lambda b, d, h, c: (b, d, h, 0, 0))
    vec_spec = pl.BlockSpec((1, wl), lambda b, d, h, c: (0, h))
    return pl.pallas_call(
        functools.partial(_scan_kernel, L=L, n=n, nb=nb, hpb=hpb),
        grid=(B, 2, HEADS // hpb, nb),
        in_specs=[row_spec, row_spec, row_spec, dir_spec, dir_spec, st_spec, vec_spec, vec_spec, vec_spec],
        out_specs=[dir_spec, dir_spec, st_spec],
        out_shape=[jax.ShapeDtypeStruct((2, B, T, D), F32), jax.ShapeDtypeStruct((2, B, T, D), F32),
                   jax.ShapeDtypeStruct((B, 2, HEADS, HEAD, HEAD), F32)],
        scratch_shapes=[pltpu.VMEM((hpb, HEAD, HEAD), F32),
                        pltpu.VMEM((hpb, tb, HEAD), F32),
                        pltpu.VMEM((hpb, tb, HEAD), F32),
                        pltpu.VMEM((hpb, n * LANES, HEAD), F32),
                        pltpu.VMEM((tb, wl), F32)],
        compiler_params=_params("parallel", "parallel", "parallel", "arbitrary"),
        name="rwkv_scan",
    )(r, k, v, lw, a, s0, p["k_k"], p["k_a"], p["r_k"])


def _seg_mean(x, e_ref, et_ref):
    xh, xl = _split2(x)
    s = (_dot(xh, e_ref[...]) + _dot(xl, e_ref[...])) * (1.0 / HEAD)
    sh, slo = _split2(s)
    return _dot(sh, et_ref[...]) + _dot(slo, et_ref[...])


def _rwkv_post_kernel(y_ref, bon_ref, gate_ref, x_ref, mod_ref, lnw_ref, lnb_ref, e_ref, et_ref, wo_ref, o_ref):
    y = y_ref[0, 0] + y_ref[1, 0]
    yc = y - _seg_mean(y, e_ref, et_ref)
    var = _seg_mean(yc * yc, e_ref, et_ref)
    yn = yc * lax.rsqrt(var + GN_EPS) * lnw_ref[...] + lnb_ref[...]
    o = (yn + (bon_ref[0, 0] + bon_ref[1, 0])) * gate_ref[0]
    o_ref[0] = x_ref[0] + mod_ref[0, 2:3, :] * _dot(o.astype(BF16), wo_ref[...])


def _rwkv_post_call(y, bon, gate, x, mod, p, seg, tm):
    B, T, D = x.shape
    row_spec = pl.BlockSpec((1, tm, D), lambda b, i: (b, i, 0))
    dir_spec = pl.BlockSpec((2, 1, tm, D), lambda b, i: (0, b, i, 0))
    return pl.pallas_call(
        _rwkv_post_kernel,
        grid=(B, T // tm),
        in_specs=[dir_spec, dir_spec, row_spec, row_spec, pl.BlockSpec((1, 6, D), lambda b, i: (b, 0, 0)),
                  _const_spec((1, D)), _const_spec((1, D)), _const_spec((D, LANES)), _const_spec((LANES, D)),
                  _const_spec((D, D))],
        out_specs=row_spec,
        out_shape=jax.ShapeDtypeStruct((B, T, D), F32),
        compiler_params=_params("parallel", "parallel"),
        name="rwkv_post",
    )(y, bon, gate, x, mod, p["ln_w"], p["ln_b"], seg[0], seg[1], p["w_o"])


def _mlp_kernel(x_ref, mod_ref, g_ref, w1_ref, w2_ref, o_ref, *, nchunk):
    x = x_ref[0]
    h = _norm_mod(x, g_ref[...], mod_ref[0, 3:4, :], mod_ref[0, 4:5, :]).astype(BF16)
    fc = D_FF // nchunk
    acc = jnp.zeros(x.shape, F32)
    for j in range(nchunk):
        hid = jnp.maximum(_dot(h, w1_ref[:, j * fc:(j + 1) * fc]), 0.0)
        acc = acc + _dot((hid * hid).astype(BF16), w2_ref[j * fc:(j + 1) * fc, :])
    o_ref[0] = x + mod_ref[0, 5:6, :] * acc


def _mlp_call(x, mod, g, w1, w2, tm):
    B, T, D = x.shape
    row_spec = pl.BlockSpec((1, tm, D), lambda b, i: (b, i, 0))
    return pl.pallas_call(
        functools.partial(_mlp_kernel, nchunk=4),
        grid=(B, T // tm),
        in_specs=[row_spec, pl.BlockSpec((1, 6, D), lambda b, i: (b, 0, 0)), _const_spec((1, D)),
                  _const_spec((D, D_FF)), _const_spec((D_FF, D))],
        out_specs=row_spec,
        out_shape=jax.ShapeDtypeStruct((B, T, D), F32),
        compiler_params=_params("parallel", "parallel"),
        name="mlp",
    )(x, mod, g, w1, w2)


def _na_qkv_kernel(x_ref, mod_ref, g_ref, w_ref, o_ref):
    h = _norm_mod(x_ref[0], g_ref[...], mod_ref[0, 0:1, :], mod_ref[0, 1:2, :]).astype(BF16)
    o_ref[0] = _dot(h, w_ref[...])


def _na_qkv_call(x, mod, g, w, tm):
    B, T, D = x.shape
    return pl.pallas_call(
        _na_qkv_kernel,
        grid=(B, T // tm),
        in_specs=[pl.BlockSpec((1, tm, D), lambda b, i: (b, i, 0)), pl.BlockSpec((1, 6, D), lambda b, i: (b, 0, 0)),
                  _const_spec((1, D)), _const_spec((D, 3 * D))],
        out_specs=pl.BlockSpec((1, tm, 3 * D), lambda b, i: (b, i, 0)),
        out_shape=jax.ShapeDtypeStruct((B, T, 3 * D), F32),
        compiler_params=_params("parallel", "parallel"),
        name="na_qkv",
    )(x, mod, g, w)


def _head_norm(x, g):
    return x * lax.rsqrt(jnp.mean(x * x, axis=-1, keepdims=True) + NORM_EPS) * g


def _na_ctx_kernel(q_ref, k_ref, v_ref, qg_ref, kg_ref, o_ref, kc_ref, vc_ref):
    q2, k2, v2 = q_ref[0], k_ref[0], v_ref[0]
    for hh in range(HEADS_PER_TILE):
        sl = slice(hh * HEAD, (hh + 1) * HEAD)
        q = _head_norm(q2[:, sl], qg_ref[...])
        k = _head_norm(k2[:, sl], kg_ref[...])
        v = v2[:, sl]
        kc_ref[0, hh] = k
        vc_ref[0, hh] = v
        s = _dot(q.astype(BF16), k.astype(BF16), NT) * ATTN_SCALE
        e = jnp.exp(s - jnp.max(s, axis=-1, keepdims=True))
        o = _dot(e.astype(BF16), v.astype(BF16)) / jnp.sum(e, axis=-1, keepdims=True)
        o_ref[0, :, sl] = o


def _qkv_specs(T):
    return [pl.BlockSpec((1, T, LANES), lambda b, h, part=part: (b, 0, part * N_HEAD_TILES + h)) for part in range(3)]


def _na_ctx_call(qkv, qg, kg):
    B, T, _ = qkv.shape
    cache_spec = pl.BlockSpec((1, HEADS_PER_TILE, T, HEAD), lambda b, h: (b, h, 0, 0))
    cache_shape = jax.ShapeDtypeStruct((B, HEADS, T, HEAD), F32)
    return pl.pallas_call(
        _na_ctx_kernel,
        grid=(B, N_HEAD_TILES),
        in_specs=_qkv_specs(T) + [_const_spec((1, HEAD)), _const_spec((1, HEAD))],
        out_specs=[pl.BlockSpec((1, T, LANES), lambda b, h: (b, 0, h)), cache_spec, cache_spec],
        out_shape=[jax.ShapeDtypeStruct((B, T, D_MODEL), F32), cache_shape, cache_shape],
        compiler_params=_params("parallel", "parallel"),
        name="na_ctx",
    )(qkv, qkv, qkv, qg, kg)


def _na_lat_kernel(q_ref, k_ref, v_ref, kc_ref, vc_ref, bias_ref, qg_ref, kg_ref, o_ref, qn_ref, kn_ref, vn_ref,
                   *, rows):
    q2, k2, v2 = q_ref[0], k_ref[0], v_ref[0]
    for hh in range(HEADS_PER_TILE):
        sl = slice(hh * HEAD, (hh + 1) * HEAD)
        qn_ref[hh] = _head_norm(q2[:, sl], qg_ref[...]).astype(BF16)
        kn_ref[hh] = _head_norm(k2[:, sl], kg_ref[...]).astype(BF16)
        vn_ref[hh] = v2[:, sl].astype(BF16)
    win = WIN_ROWS * GRID_W

    def row_block(r, carry):
        rs = jnp.clip(r - WIN_ROWS // 2, 0, rows - WIN_ROWS)
        var = rs - r + WIN_ROWS - 1
        q0 = pl.multiple_of(r * GRID_W, GRID_W)
        k0 = pl.multiple_of(rs * GRID_W, GRID_W)
        for hh in range(HEADS_PER_TILE):
            q = qn_ref[hh, pl.ds(q0, GRID_W), :]
            kw = kn_ref[hh, pl.ds(k0, win), :]
            vw = vn_ref[hh, pl.ds(k0, win), :]
            s_w = _dot(q, kw, NT) * ATTN_SCALE + bias_ref[hh, var]
            s_c = _dot(q, kc_ref[0, hh].astype(BF16), NT) * ATTN_SCALE
            m = jnp.maximum(jnp.max(s_w, axis=-1, keepdims=True), jnp.max(s_c, axis=-1, keepdims=True))
            e_w = jnp.exp(s_w - m)
            e_c = jnp.exp(s_c - m)
            den = jnp.sum(e_w, axis=-1, keepdims=True) + jnp.sum(e_c, axis=-1, keepdims=True)
            o = _dot(e_w.astype(BF16), vw) + _dot(e_c.astype(BF16), vc_ref[0, hh].astype(BF16))
            o_ref[0, pl.ds(q0, GRID_W), hh * HEAD:(hh + 1) * HEAD] = o / den
        return carry

    lax.fori_loop(0, rows, row_block, 0)


def _na_lat_call(qkv, k_ctx, v_ctx, bias, qg, kg):
    B, T, _ = qkv.shape
    P = k_ctx.shape[2]
    rows = T // GRID_W
    ctx_spec = pl.BlockSpec((1, HEADS_PER_TILE, P, HEAD), lambda b, h: (b, h, 0, 0))
    return pl.pallas_call(
        functools.partial(_na_lat_kernel, rows=rows),
        grid=(B, N_HEAD_TILES),
        in_specs=_qkv_specs(T) + [
            ctx_spec, ctx_spec,
            pl.BlockSpec((HEADS_PER_TILE, WIN_ROWS, GRID_W, WIN_ROWS * GRID_W), lambda b, h: (h, 0, 0, 0)),
            _const_spec((1, HEAD)), _const_spec((1, HEAD))],
        out_specs=pl.BlockSpec((1, T, LANES), lambda b, h: (b, 0, h)),
        out_shape=jax.ShapeDtypeStruct((B, T, D_MODEL), F32),
        scratch_shapes=[pltpu.VMEM((HEADS_PER_TILE, T, HEAD), BF16)] * 3,
        compiler_params=_params("parallel", "parallel"),
        name="na_lat",
    )(qkv, qkv, qkv, k_ctx, v_ctx, bias, qg, kg)


def _na_bias_table(rpb):
    n_col = 2 * WIN_COLS - 1
    qc = np.arange(GRID_W)[:, None]
    kc = np.arange(GRID_W)[None, :]
    col = np.clip(kc - qc + WIN_COLS - 1, 0, n_col - 1)
    ws = np.clip(qc - WIN_COLS // 2, 0, GRID_W - WIN_COLS)
    valid = (kc >= ws) & (kc < ws + WIN_COLS)
    onehot = ((col[None] == np.arange(n_col)[:, None, None]) & valid[None]).astype(np.float32)
    mask_bias = np.where(valid, 0.0, NEG_BIG).astype(np.float32)
    rows = jnp.stack([rpb[:, al:al + WIN_ROWS, :] for al in range(WIN_ROWS)], axis=1)
    tab = jnp.einsum("hajc,cqk->haqjk", rows, jnp.asarray(onehot), precision=lax.Precision.HIGHEST)
    tab = tab + jnp.asarray(mask_bias)[None, None, :, None, :]
    return tab.reshape(HEADS, WIN_ROWS, GRID_W, WIN_ROWS * GRID_W)


def _out_proj_kernel(a_ref, x_ref, mod_ref, w_ref, o_ref):
    o_ref[0] = x_ref[0] + mod_ref[0, 2:3, :] * _dot(a_ref[0].astype(BF16), w_ref[...])


def _out_proj_call(a, x, mod, w, tm):
    B, T, D = x.shape
    row_spec = pl.BlockSpec((1, tm, D), lambda b, i: (b, i, 0))
    return pl.pallas_call(
        _out_proj_kernel,
        grid=(B, T // tm),
        in_specs=[row_spec, row_spec, pl.BlockSpec((1, 6, D), lambda b, i: (b, 0, 0)), _const_spec((D, D))],
        out_specs=row_spec,
        out_shape=jax.ShapeDtypeStruct((B, T, D), F32),
        compiler_params=_params("parallel", "parallel"),
        name="out_proj",
    )(a, x, mod, w)


def _pad_lora_in(w):
    return jnp.pad(w, ((0, 0), (0, LORA_PAD - w.shape[1])))


def _rwkv_layer_params(i, mu, w_rkv, w_o, w0, w1, w2, a0, a1, a2, g1, g2, k_k, k_a, r_k, ln_w, ln_b):
    rank = w1.shape[-1]
    wl1 = jnp.stack([jnp.concatenate([w1[i, 0], w1[i, 1]], axis=1),
                     jnp.concatenate([a1[i, 0], a1[i, 1]], axis=1),
                     _pad_lora_in(g1[i])]).astype(BF16)
    zeros = jnp.zeros((rank, D_MODEL), F32)
    w2p = jnp.stack([jnp.concatenate([w2[i, 0], zeros]), jnp.concatenate([zeros, w2[i, 1]])]).astype(BF16)
    a2p = jnp.stack([jnp.concatenate([a2[i, 0], zeros]), jnp.concatenate([zeros, a2[i, 1]])]).astype(BF16)
    g2p = jnp.pad(g2[i], ((0, LORA_PAD - g2.shape[1]), (0, 0))).astype(BF16)
    return dict(mu=mu[i], w_rkv=w_rkv[i].astype(BF16), wl1=wl1, w2p=w2p, w0=w0[i], a2p=a2p, a0=a0[i], g2=g2p,
                k_k=k_k[i][None], k_a=k_a[i][None], r_k=r_k[i].reshape(1, D_MODEL),
                ln_w=ln_w[i][None], ln_b=ln_b[i][None], w_o=w_o[i].astype(BF16))


def _segment_matrices():
    e = (np.arange(D_MODEL)[:, None] // HEAD == np.arange(LANES)[None, :]).astype(np.float32)
    return jnp.asarray(e, BF16), jnp.asarray(e.T, BF16)


def kernel(x_prompt, x_sample, state_rwkv, cache_na_k, cache_na_v, c, c_ctx, norm_g, ada_w, ada_b, mlp_w1, mlp_w2, rwkv_mu, rwkv_w_rkv, rwkv_w_o, rwkv_w0, rwkv_w1, rwkv_w2, rwkv_a0, rwkv_a1, rwkv_a2, rwkv_g1, rwkv_g2, rwkv_k_k, rwkv_k_a, rwkv_r_k, rwkv_ln_w, rwkv_ln_b, na_w_qkv, na_w_o, na_q_g, na_k_g, na_rpb):
    n_dec = c.shape[0]
    bp = x_prompt.shape[0]
    cond_rows = 16
    cond = jnp.zeros((cond_rows, D_MODEL), F32).at[:n_dec].set(c).at[n_dec].set(c_ctx)
    mods = _ada_call(cond, ada_w, ada_b)
    mod_lat = mods[:, :n_dec].reshape(DEPTH, n_dec, 6, D_MODEL)
    mod_ctx = jnp.broadcast_to(mods[:, n_dec].reshape(DEPTH, 1, 6, D_MODEL), (DEPTH, bp, 6, D_MODEL))

    rwkv_raw = (rwkv_mu, rwkv_w_rkv, rwkv_w_o, rwkv_w0, rwkv_w1, rwkv_w2, rwkv_a0, rwkv_a1, rwkv_a2,
                rwkv_g1, rwkv_g2, rwkv_k_k, rwkv_k_a, rwkv_r_k, rwkv_ln_w, rwkv_ln_b)
    n_rwkv = rwkv_mu.shape[0]
    n_na = na_w_qkv.shape[0]
    rwkv_p = [_rwkv_layer_params(i, *rwkv_raw) for i in range(n_rwkv)]
    seg = _segment_matrices()
    w1_bf = mlp_w1.astype(BF16)
    w2_bf = mlp_w2.astype(BF16)
    wqkv_bf = na_w_qkv.astype(BF16)
    wo_bf = na_w_o.astype(BF16)
    bias_tabs = [_na_bias_table(na_rpb[i]) for i in range(n_na)]

    def run(x, mod_all, tm, s0_fn, attn_fn):
        states = []
        for l in range(DEPTH):
            i = l // 2
            mod = mod_all[l]
            if l % 2 == 0:
                p = rwkv_p[i]
                r, k, v, gate, lw, a = _rwkv_proj_call(x, mod, norm_g[l, 0][None], p, tm)
                y, bon, s_fin = _rwkv_scan_call(r, k, v, lw, a, s0_fn(i), p, 256, 4)
                states.append(s_fin)
                x = _rwkv_post_call(y, bon, gate, x, mod, p, seg, tm)
            else:
                qkv = _na_qkv_call(x, mod, norm_g[l, 0][None], wqkv_bf[i], tm)
                o = attn_fn(i, qkv)
                x = _out_proj_call(o, x, mod, wo_bf[i], tm)
            x = _mlp_call(x, mod, norm_g[l, 1][None], w1_bf[l], w2_bf[l], tm)
        return x, states

    new_k, new_v = [], []

    def ctx_attn(i, qkv):
        o, k_c, v_c = _na_ctx_call(qkv, na_q_g[i][None], na_k_g[i][None])
        new_k.append(k_c)
        new_v.append(v_c)
        return o

    def lat_attn(i, qkv):
        return _na_lat_call(qkv, cache_na_k[:, i], cache_na_v[:, i], bias_tabs[i], na_q_g[i][None], na_k_g[i][None])

    zero_state = jnp.zeros((bp, 2, HEADS, HEAD, HEAD), F32)
    y_prompt, new_states = run(x_prompt, mod_ctx, 256, lambda i: zero_state, ctx_attn)
    y_sample, _ = run(x_sample, mod_lat, 256, lambda i: state_rwkv[:, i], lat_attn)
    return (y_prompt, y_sample, jnp.stack(new_states, axis=1), jnp.stack(new_k, axis=1), jnp.stack(new_v, axis=1))
```

```python
import functools

import jax
import jax.numpy as jnp
import numpy as np
from jax import lax
from jax.experimental import pallas as pl
from jax.experimental.pallas import tpu as pltpu

F32 = jnp.float32
BF16 = jnp.bfloat16

D_MODEL = 1024
DEPTH = 4
HEADS = 16
HEAD = 64
LANES = 128
HEADS_PER_TILE = LANES // HEAD
N_HEAD_TILES = HEADS // HEADS_PER_TILE
SUBLANES = 8
D_FF = 4 * D_MODEL
LORA_PAD = 128
GRID_W = 64
WIN_ROWS = 8
WIN_COLS = 16
NORM_EPS = 1e-6
GN_EPS = 64e-5
ATTN_SCALE = HEAD ** -0.5
NEG_BIG = -1e30
DECAY_SCALE = float(np.exp(-0.5))
SCAN_CHUNK = 64
VMEM_LIMIT = 56 * 1024 * 1024

NN = ((1,), (0,))
NT = ((1,), (1,))
TN = ((0,), (0,))


def _dot(a, b, dims=NN):
    return lax.dot_general(a, b, (dims, ((), ())), preferred_element_type=F32)


def _split2(x):
    hi = x.astype(BF16)
    lo = (x - hi.astype(F32)).astype(BF16)
    return hi, lo


def _split3(x):
    hi = x.astype(BF16)
    r1 = x - hi.astype(F32)
    mid = r1.astype(BF16)
    lo = (r1 - mid.astype(F32)).astype(BF16)
    return hi, mid, lo


def _dot3(a, b, dims=NN):
    ah, al = _split2(a)
    bh, bl = _split2(b)
    return _dot(ah, bh, dims) + (_dot(ah, bl, dims) + _dot(al, bh, dims))


def _dot_exact_lhs(a_bf16, b, dims=NN):
    bh, bm, bl = _split3(b)
    return _dot(a_bf16, bh, dims) + (_dot(a_bf16, bm, dims) + _dot(a_bf16, bl, dims))


def _params(*sem):
    return pltpu.CompilerParams(dimension_semantics=sem, vmem_limit_bytes=VMEM_LIMIT)


def _norm_mod(x, g, shift, scale):
    ms = jnp.mean(x * x, axis=-1, keepdims=True)
    return (x * lax.rsqrt(ms + NORM_EPS) * g) * (1.0 + scale) + shift


def _const_spec(shape):
    nd = len(shape)
    return pl.BlockSpec(shape, lambda *_: (0,) * nd)


def _ada_kernel(c_ref, w_ref, b_ref, o_ref):
    c = c_ref[...]
    s = c * jax.nn.sigmoid(c)
    o_ref[0] = _dot3(s, w_ref[0]) + b_ref[0]


def _ada_call(cond, ada_w, ada_b):
    rows = cond.shape[0]
    tn = 768
    n = 6 * D_MODEL
    return pl.pallas_call(
        _ada_kernel,
        grid=(DEPTH, n // tn),
        in_specs=[
            pl.BlockSpec((rows, D_MODEL), lambda l, j: (0, 0)),
            pl.BlockSpec((1, D_MODEL, tn), lambda l, j: (l, 0, j)),
            pl.BlockSpec((1, 1, tn), lambda l, j: (l, 0, j)),
        ],
        out_specs=pl.BlockSpec((1, rows, tn), lambda l, j: (l, 0, j)),
        out_shape=jax.ShapeDtypeStruct((DEPTH, rows, n), F32),
        compiler_params=_params("parallel", "parallel"),
        name="ada_mod",
    )(cond, ada_w, ada_b.reshape(DEPTH, 1, n))


def _rwkv_proj_kernel(x_ref, xp_ref, xn_ref, mod_ref, g_ref, mu_ref, wrkv_ref, wl1_ref, w2p_ref, w0_ref,
                      a2p_ref, a0_ref, g2_ref, r_ref, k_ref, v_ref, gate_ref, lw_ref, a_ref, *, tm, nt):
    i = pl.program_id(1)
    g = g_ref[...]
    shift = mod_ref[0, 0:1, :]
    scale = mod_ref[0, 1:2, :]
    h = _norm_mod(x_ref[0], g, shift, scale)
    hp = _norm_mod(xp_ref[0], g, shift, scale)[SUBLANES - 1:SUBLANES, :]
    hn = _norm_mod(xn_ref[0], g, shift, scale)[0:1, :]
    hp = jnp.where(i == 0, 0.0, hp)
    hn = jnp.where(i == nt - 1, 0.0, hn)
    row = lax.broadcasted_iota(jnp.int32, (tm, 1), 0)
    prev = jnp.where(row == 0, hp, pltpu.roll(h, 1, 0))
    nxt = jnp.where(row == tm - 1, hn, pltpu.roll(h, tm - 1, 0))
    delta = 0.5 * (prev + nxt) - h

    def mix(j):
        return (h + delta * mu_ref[j:j + 1, :]).astype(BF16)

    r_ref[0] = _dot(mix(0), wrkv_ref[0])
    k_ref[0] = _dot(mix(1), wrkv_ref[1])
    v_ref[0] = _dot(mix(2), wrkv_ref[2])
    lw = jnp.tanh(_dot(mix(3), wl1_ref[0])).astype(BF16)
    la = _dot(mix(4), wl1_ref[1]).astype(BF16)
    lg = jax.nn.sigmoid(_dot(mix(5), wl1_ref[2])).astype(BF16)
    gate_ref[0] = _dot(lg, g2_ref[...])
    for d in range(2):
        z = w0_ref[d:d + 1, :] + _dot(lw, w2p_ref[d])
        lw_ref[d, 0] = -DECAY_SCALE * jax.nn.sigmoid(z)
        a_ref[d, 0] = jax.nn.sigmoid(a0_ref[d:d + 1, :] + _dot(la, a2p_ref[d]))


def _rwkv_proj_call(x, mod, g, p, tm):
    B, T, D = x.shape
    nt = T // tm
    tb = tm // SUBLANES
    row_spec = pl.BlockSpec((1, tm, D), lambda b, i: (b, i, 0))
    dir_spec = pl.BlockSpec((2, 1, tm, D), lambda b, i: (0, b, i, 0))
    out_bt = jax.ShapeDtypeStruct((B, T, D), F32)
    out_dir = jax.ShapeDtypeStruct((2, B, T, D), F32)
    return pl.pallas_call(
        functools.partial(_rwkv_proj_kernel, tm=tm, nt=nt),
        grid=(B, nt),
        in_specs=[
            row_spec,
            pl.BlockSpec((1, SUBLANES, D), lambda b, i: (b, jnp.maximum(i * tb - 1, 0), 0)),
            pl.BlockSpec((1, SUBLANES, D), lambda b, i: (b, jnp.minimum((i + 1) * tb, T // SUBLANES - 1), 0)),
            pl.BlockSpec((1, 6, D), lambda b, i: (b, 0, 0)),
            _const_spec((1, D)),
            _const_spec((6, D)),
            _const_spec((3, D, D)),
            _const_spec((3, D, LORA_PAD)),
            _const_spec((2, LORA_PAD, D)),
            _const_spec((2, D)),
            _const_spec((2, LORA_PAD, D)),
            _const_spec((2, D)),
            _const_spec((LORA_PAD, D)),
        ],
        out_specs=[row_spec, row_spec, row_spec, row_spec, dir_spec, dir_spec],
        out_shape=[out_bt, out_bt, out_bt, out_bt, out_dir, out_dir],
        compiler_params=_params("parallel", "parallel"),
        name="rwkv_proj",
    )(x, x, x, mod, g, p["mu"], p["w_rkv"], p["wl1"], p["w2p"], p["w0"], p["a2p"], p["a0"], p["g2"])


def _scan_kernel(r_ref, k_ref, v_ref, lw_ref, a_ref, s0_ref, kk_ref, ka_ref, rk_ref,
                 y_ref, bon_ref, sf_ref, s_ref, qt_ref, y0_ref, gh_ref, et_ref, *, L, n, nb, hpb):
    d = pl.program_id(1)
    blk = pl.program_id(3)
    TB = n * L

    @pl.when(blk == 0)
    def _():
        s_ref[...] = s0_ref[0, 0]

    sign = 1 - 2 * d
    row = lax.broadcasted_iota(jnp.int32, (TB, TB), 0)
    col = lax.broadcasted_iota(jnp.int32, (TB, TB), 1)
    same = (row // L) == (col // L)
    tri = jnp.where(same & ((row - col) * sign >= 0), 1.0, 0.0).astype(BF16)
    ones_blk = jnp.where(same, 1.0, 0.0).astype(BF16)
    lrow = lax.broadcasted_iota(jnp.int32, (LANES, LANES), 0)
    lcol = lax.broadcasted_iota(jnp.int32, (LANES, LANES), 1)
    head_ones = jnp.where((lrow // HEAD) == (lcol // HEAD), 1.0, 0.0).astype(BF16)
    gorder = ((lrow % L) - (lcol % L)) * sign
    gmask = gorder + jnp.where(lrow < L, 0, 1) > 0
    eye = jnp.where(lax.broadcasted_iota(jnp.int32, (L, L), 0) == lax.broadcasted_iota(jnp.int32, (L, L), 1), 1.0, 0.0)

    def seg_sum(x):
        xh, xl = _split2(x)
        tiles = [slice(t * LANES, (t + 1) * LANES) for t in range(hpb // HEADS_PER_TILE)]
        return jnp.concatenate([_dot(xh[:, t], head_ones) + _dot(xl[:, t], head_ones) for t in tiles], axis=1)

    r2, k2, v2 = r_ref[0], k_ref[0], v_ref[0]
    lw2, a2 = lw_ref[0, 0], a_ref[0, 0]
    kkr = k2 * kk_ref[...]
    kk = kkr * lax.rsqrt(seg_sum(kkr * kkr) + 1e-12)
    kd = k2 * (1.0 + (a2 - 1.0) * ka_ref[...])
    b2 = kk * a2
    bon_ref[0, 0] = seg_sum(r2 * kd * rk_ref[...]) * v2

    cum = _dot_exact_lhs(tri, lw2)
    tot = _dot_exact_lhs(ones_blk, lw2)
    al2 = -kk * jnp.exp(cum - lw2)
    rh2 = r2 * jnp.exp(cum)
    e_neg = jnp.exp(-cum)
    bc2 = b2 * e_neg
    kc2 = kd * e_neg
    e_tail = jnp.exp(tot - cum)
    be2 = b2 * e_tail
    ke2 = kd * e_tail
    et_ref[...] = jnp.exp(tot)

    chains = [(j, hh) for j in range(n) for hh in range(hpb)]
    nch = range(len(chains))

    def part(x, i):
        j, hh = chains[i]
        return x[j * L:(j + 1) * L, hh * HEAD:(hh + 1) * HEAD]

    zeros_h = jnp.zeros((L, HEAD), F32)
    al = [part(al2, i) for i in nch]
    rh = [part(rh2, i) for i in nch]
    v = [part(v2, i) for i in nch]
    lhs = [jnp.concatenate([al[i], rh[i]], axis=0).astype(BF16) for i in nch]
    rhs = [jnp.concatenate([part(bc2, i), part(kc2, i)], axis=0).astype(BF16) for i in nch]
    bek = [jnp.concatenate([part(be2, i), part(ke2, i)], axis=0).astype(BF16) for i in nch]
    g = [jnp.where(gmask, _dot(lhs[i], rhs[i], NT), 0.0) for i in nch]
    gb = [x.astype(BF16) for x in g]
    w = [_dot(gb[i][:L], jnp.concatenate([zeros_h, v[i]], axis=0).astype(BF16)) for i in nch]
    pw = [gb[i][:L, :L] for i in nch]
    t = [eye + g[i][:L, :L] for i in nch]
    span = 2
    while span < L:
        pw = [_dot(pw[i], pw[i]).astype(BF16) for i in nch]
        t = [t[i] + _dot(t[i].astype(BF16), pw[i]) for i in nch]
        span *= 2
    tx = [_dot(t[i].astype(BF16), jnp.concatenate([al[i], w[i]], axis=1).astype(BF16)) for i in nch]
    z = [jnp.concatenate([tx[i], jnp.concatenate([zeros_h, v[i]], axis=1)], axis=0).astype(BF16) for i in nch]
    qy = [_dot(gb[i][L:], z[i]) for i in nch]
    gh = [_dot(z[i], bek[i], TN) for i in nch]
    for i in nch:
        j, hh = chains[i]
        qt_ref[hh, j * L:(j + 1) * L, :] = rh[i] + qy[i][:, :HEAD]
        y0_ref[hh, j * L:(j + 1) * L, :] = qy[i][:, HEAD:]
        gh_ref[hh, j * LANES:(j + 1) * LANES, :] = gh[i]

    def chunk_step(jj, carry):
        cj = jj + d * (n - 1 - 2 * jj)
        r0 = pl.multiple_of(cj * L, L)
        g0 = pl.multiple_of(cj * LANES, LANES)
        s = [s_ref[hh] for hh in range(hpb)]
        sb = [x.astype(BF16) for x in s]
        y = [_dot(qt_ref[hh, pl.ds(r0, L), :].astype(BF16), sb[hh], NT) for hh in range(hpb)]
        sg = [_dot(sb[hh], gh_ref[hh, pl.ds(g0, HEAD), :].astype(BF16)) for hh in range(hpb)]
        for hh in range(hpb):
            y_ref[0, 0, pl.ds(r0, L), hh * HEAD:(hh + 1) * HEAD] = y[hh] + y0_ref[hh, pl.ds(r0, L), :]
            decay = et_ref[pl.ds(r0, SUBLANES), hh * HEAD:(hh + 1) * HEAD][0:1, :]
            s_ref[hh] = s[hh] * decay + sg[hh] + gh_ref[hh, pl.ds(g0 + HEAD, HEAD), :]
        return carry

    lax.fori_loop(0, n, chunk_step, 0, unroll=True)

    @pl.when(blk == nb - 1)
    def _():
        sf_ref[0, 0] = s_ref[...]


def _rwkv_scan_call(r, k, v, lw, a, s0, p, tb, hpb):
    B, T, D = r.shape
    L = SCAN_CHUNK
    n = tb // L
    nb = T // tb
    wl = hpb * HEAD

    def bidx(d, c):
        return c + d * (nb - 1 - 2 * c)

    row_spec = pl.BlockSpec((1, tb, wl), lambda b, d, h, c: (b, bidx(d, c), h))
    dir_spec = pl.BlockSpec((1, 1, tb, wl), lambda b, d, h, c: (d, b, bidx(d, c), h))
    st_spec = pl.BlockSpec((1, 1, hpb, HEAD, HEAD), lambda b, d, h, c: (b, d, h, 0, 0))
    vec_spec = pl.BlockSpec((1, wl), lambda b, d, h, c: (0, h))
    return pl.pallas_call(
        functools.partial(_scan_kernel, L=L, n=n, nb=nb, hpb=hpb),
        grid=(B, 2, HEADS // hpb, nb),
        in_specs=[row_spec, row_spec, row_spec, dir_spec, dir_spec, st_spec, vec_spec, vec_spec, vec_spec],
        out_specs=[dir_spec, dir_spec, st_spec],
        out_shape=[jax.ShapeDtypeStruct((2, B, T, D), F32), jax.ShapeDtypeStruct((2, B, T, D), F32),
                   jax.ShapeDtypeStruct((B, 2, HEADS, HEAD, HEAD), F32)],
        scratch_shapes=[pltpu.VMEM((hpb, HEAD, HEAD), F32),
                        pltpu.VMEM((hpb, tb, HEAD), F32),
                        pltpu.VMEM((hpb, tb, HEAD), F32),
                        pltpu.VMEM((hpb, n * LANES, HEAD), F32),
                        pltpu.VMEM((tb, wl), F32)],
        compiler_params=_params("parallel", "parallel", "parallel", "arbitrary"),
        name="rwkv_scan",
    )(r, k, v, lw, a, s0, p["k_k"], p["k_a"], p["r_k"])


def _seg_mean(x, e_ref, et_ref):
    xh, xl = _split2(x)
    s = (_dot(xh, e_ref[...]) + _dot(xl, e_ref[...])) * (1.0 / HEAD)
    sh, slo = _split2(s)
    return _dot(sh, et_ref[...]) + _dot(slo, et_ref[...])


def _rwkv_post_kernel(y_ref, bon_ref, gate_ref, x_ref, mod_ref, lnw_ref, lnb_ref, e_ref, et_ref, wo_ref, o_ref):
    y = y_ref[0, 0] + y_ref[1, 0]
    yc = y - _seg_mean(y, e_ref, et_ref)
    var = _seg_mean(yc * yc, e_ref, et_ref)
    yn = yc * lax.rsqrt(var + GN_EPS) * lnw_ref[...] + lnb_ref[...]
    o = (yn + (bon_ref[0, 0] + bon_ref[1, 0])) * gate_ref[0]
    o_ref[0] = x_ref[0] + mod_ref[0, 2:3, :] * _dot(o.astype(BF16), wo_ref[...])


def _rwkv_post_call(y, bon, gate, x, mod, p, seg, tm):
    B, T, D = x.shape
    row_spec = pl.BlockSpec((1, tm, D), lambda b, i: (b, i, 0))
    dir_spec = pl.BlockSpec((2, 1, tm, D), lambda b, i: (0, b, i, 0))
    return pl.pallas_call(
        _rwkv_post_kernel,
        grid=(B, T // tm),
        in_specs=[dir_spec, dir_spec, row_spec, row_spec, pl.BlockSpec((1, 6, D), lambda b, i: (b, 0, 0)),
                  _const_spec((1, D)), _const_spec((1, D)), _const_spec((D, LANES)), _const_spec((LANES, D)),
                  _const_spec((D, D))],
        out_specs=row_spec,
        out_shape=jax.ShapeDtypeStruct((B, T, D), F32),
        compiler_params=_params("parallel", "parallel"),
        name="rwkv_post",
    )(y, bon, gate, x, mod, p["ln_w"], p["ln_b"], seg[0], seg[1], p["w_o"])


def _mlp_kernel(x_ref, mod_ref, g_ref, w1_ref, w2_ref, o_ref, *, nchunk):
    x = x_ref[0]
    h = _norm_mod(x, g_ref[...], mod_ref[0, 3:4, :], mod_ref[0, 4:5, :]).astype(BF16)
    fc = D_FF // nchunk
    acc = jnp.zeros(x.shape, F32)
    for j in range(nchunk):
        hid = jnp.maximum(_dot(h, w1_ref[:, j * fc:(j + 1) * fc]), 0.0)
        acc = acc + _dot((hid * hid).astype(BF16), w2_ref[j * fc:(j + 1) * fc, :])
    o_ref[0] = x + mod_ref[0, 5:6, :] * acc


def _mlp_call(x, mod, g, w1, w2, tm):
    B, T, D = x.shape
    row_spec = pl.BlockSpec((1, tm, D), lambda b, i: (b, i, 0))
    return pl.pallas_call(
        functools.partial(_mlp_kernel, nchunk=4),
        grid=(B, T // tm),
        in_specs=[row_spec, pl.BlockSpec((1, 6, D), lambda b, i: (b, 0, 0)), _const_spec((1, D)),
                  _const_spec((D, D_FF)), _const_spec((D_FF, D))],
        out_specs=row_spec,
        out_shape=jax.ShapeDtypeStruct((B, T, D), F32),
        compiler_params=_params("parallel", "parallel"),
        name="mlp",
    )(x, mod, g, w1, w2)


def _na_qkv_kernel(x_ref, mod_ref, g_ref, w_ref, o_ref):
    h = _norm_mod(x_ref[0], g_ref[...], mod_ref[0, 0:1, :], mod_ref[0, 1:2, :]).astype(BF16)
    o_ref[0] = _dot(h, w_ref[...])


def _na_qkv_call(x, mod, g, w, tm):
    B, T, D = x.shape
    return pl.pallas_call(
        _na_qkv_kernel,
        grid=(B, T // tm),
        in_specs=[pl.BlockSpec((1, tm, D), lambda b, i: (b, i, 0)), pl.BlockSpec((1, 6, D), lambda b, i: (b, 0, 0)),
                  _const_spec((1, D)), _const_spec((D, 3 * D))],
        out_specs=pl.BlockSpec((1, tm, 3 * D), lambda b, i: (b, i, 0)),
        out_shape=jax.ShapeDtypeStruct((B, T, 3 * D), F32),
        compiler_params=_params("parallel", "parallel"),
        name="na_qkv",
    )(x, mod, g, w)


def _head_norm(x, g):
    return x * lax.rsqrt(jnp.mean(x * x, axis=-1, keepdims=True) + NORM_EPS) * g


def _na_ctx_kernel(q_ref, k_ref, v_ref, qg_ref, kg_ref, o_ref, kc_ref, vc_ref):
    q2, k2, v2 = q_ref[0], k_ref[0], v_ref[0]
    for hh in range(HEADS_PER_TILE):
        sl = slice(hh * HEAD, (hh + 1) * HEAD)
        q = _head_norm(q2[:, sl], qg_ref[...])
        k = _head_norm(k2[:, sl], kg_ref[...])
        v = v2[:, sl]
        kc_ref[0, hh] = k
        vc_ref[0, hh] = v
        s = _dot(q.astype(BF16), k.astype(BF16), NT) * ATTN_SCALE
        e = jnp.exp(s - jnp.max(s, axis=-1, keepdims=True))
        o = _dot(e.astype(BF16), v.astype(BF16)) / jnp.sum(e, axis=-1, keepdims=True)
        o_ref[0, :, sl] = o


def _qkv_specs(T):
    return [pl.BlockSpec((1, T, LANES), lambda b, h, part=part: (b, 0, part * N_HEAD_TILES + h)) for part in range(3)]


def _na_ctx_call(qkv, qg, kg):
    B, T, _ = qkv.shape
    cache_spec = pl.BlockSpec((1, HEADS_PER_TILE, T, HEAD), lambda b, h: (b, h, 0, 0))
    cache_shape = jax.ShapeDtypeStruct((B, HEADS, T, HEAD), F32)
    return pl.pallas_call(
        _na_ctx_kernel,
        grid=(B, N_HEAD_TILES),
        in_specs=_qkv_specs(T) + [_const_spec((1, HEAD)), _const_spec((1, HEAD))],
        out_specs=[pl.BlockSpec((1, T, LANES), lambda b, h: (b, 0, h)), cache_spec, cache_spec],
        out_shape=[jax.ShapeDtypeStruct((B, T, D_MODEL), F32), cache_shape, cache_shape],
        compiler_params=_params("parallel", "parallel"),
        name="na_ctx",
    )(qkv, qkv, qkv, qg, kg)


def _na_lat_kernel(q_ref, k_ref, v_ref, kc_ref, vc_ref, bias_ref, qg_ref, kg_ref, o_ref, qn_ref, kn_ref, vn_ref,
                   *, rows, rg):
    lrow = lax.broadcasted_iota(jnp.int32, (LANES, LANES), 0)
    lcol = lax.broadcasted_iota(jnp.int32, (LANES, LANES), 1)
    head_ones = jnp.where((lrow // HEAD) == (lcol // HEAD), 1.0, 0.0).astype(BF16)

    def tile_head_norm(x, g):
        xh, xl = _split2(x * x)
        ms = (_dot(xh, head_ones) + _dot(xl, head_ones)) * (1.0 / HEAD)
        return x * lax.rsqrt(ms + NORM_EPS) * g

    qn_ref[...] = (tile_head_norm(q_ref[0], qg_ref[...]) * ATTN_SCALE).astype(BF16)
    kn_ref[...] = tile_head_norm(k_ref[0], kg_ref[...]).astype(BF16)
    vn_ref[...] = v_ref[0].astype(BF16)
    win = WIN_ROWS * GRID_W
    kc2 = jnp.concatenate([kc_ref[0, hh] for hh in range(HEADS_PER_TILE)], axis=1).astype(BF16)
    vc2 = jnp.concatenate([vc_ref[0, hh] for hh in range(HEADS_PER_TILE)], axis=1).astype(BF16)
    first_head = lax.broadcasted_iota(jnp.int32, (GRID_W, LANES), 1) < HEAD
    nrg = range(rg)

    def row_group(gi, carry):
        q0, k0, var = [], [], []
        for rr in nrg:
            r = gi * rg + rr
            rs = jnp.clip(r - WIN_ROWS // 2, 0, rows - WIN_ROWS)
            var.append(rs - r + WIN_ROWS - 1)
            q0.append(pl.multiple_of(r * GRID_W, GRID_W))
            k0.append(pl.multiple_of(rs * GRID_W, GRID_W))
        qt = [qn_ref[pl.ds(q0[rr], GRID_W), :] for rr in nrg]
        zero = jnp.zeros((GRID_W, LANES), BF16)
        q = [jnp.concatenate([jnp.where(first_head, qt[rr], zero), jnp.where(first_head, zero, qt[rr])], axis=0)
             for rr in nrg]
        s_w = [_dot(q[rr], kn_ref[pl.ds(k0[rr], win), :], NT) + bias_ref[0, var[rr]] for rr in nrg]
        s_c = [_dot(q[rr], kc2, NT) for rr in nrg]
        m = [jnp.maximum(jnp.max(s_w[rr], axis=-1, keepdims=True), jnp.max(s_c[rr], axis=-1, keepdims=True))
             for rr in nrg]
        e_w = [jnp.exp(s_w[rr] - m[rr]) for rr in nrg]
        e_c = [jnp.exp(s_c[rr] - m[rr]) for rr in nrg]
        den = [jnp.sum(e_w[rr], axis=-1, keepdims=True) + jnp.sum(e_c[rr], axis=-1, keepdims=True) for rr in nrg]
        o = [(_dot(e_w[rr].astype(BF16), vn_ref[pl.ds(k0[rr], win), :]) + _dot(e_c[rr].astype(BF16), vc2)) / den[rr]
             for rr in nrg]
        for rr in nrg:
            o_ref[0, pl.ds(q0[rr], GRID_W), :] = jnp.where(first_head, o[rr][:GRID_W], o[rr][GRID_W:])
        return carry

    lax.fori_loop(0, rows // rg, row_group, 0)


def _na_lat_call(qkv, k_ctx, v_ctx, bias, qg, kg):
    B, T, _ = qkv.shape
    P = k_ctx.shape[2]
    rows = T // GRID_W
    ctx_spec = pl.BlockSpec((1, HEADS_PER_TILE, P, HEAD), lambda b, h: (b, h, 0, 0))
    return pl.pallas_call(
        functools.partial(_na_lat_kernel, rows=rows, rg=4),
        grid=(B, N_HEAD_TILES),
        in_specs=_qkv_specs(T) + [
            ctx_spec, ctx_spec,
            pl.BlockSpec((1, WIN_ROWS, HEADS_PER_TILE * GRID_W, WIN_ROWS * GRID_W), lambda b, h: (h, 0, 0, 0)),
            _const_spec((1, LANES)), _const_spec((1, LANES))],
        out_specs=pl.BlockSpec((1, T, LANES), lambda b, h: (b, 0, h)),
        out_shape=jax.ShapeDtypeStruct((B, T, D_MODEL), F32),
        scratch_shapes=[pltpu.VMEM((T, LANES), BF16)] * 3,
        compiler_params=_params("parallel", "parallel"),
        name="na_lat",
    )(qkv, qkv, qkv, k_ctx, v_ctx, bias, jnp.tile(qg, (1, HEADS_PER_TILE)), jnp.tile(kg, (1, HEADS_PER_TILE)))


def _na_bias_table(rpb):
    n_col = 2 * WIN_COLS - 1
    qc = np.arange(GRID_W)[:, None]
    kc = np.arange(GRID_W)[None, :]
    col = np.clip(kc - qc + WIN_COLS - 1, 0, n_col - 1)
    ws = np.clip(qc - WIN_COLS // 2, 0, GRID_W - WIN_COLS)
    valid = (kc >= ws) & (kc < ws + WIN_COLS)
    onehot = ((col[None] == np.arange(n_col)[:, None, None]) & valid[None]).astype(np.float32)
    mask_bias = np.where(valid, 0.0, NEG_BIG).astype(np.float32)
    rows = jnp.stack([rpb[:, al:al + WIN_ROWS, :] for al in range(WIN_ROWS)], axis=1)
    tab = jnp.einsum("hajc,cqk->haqjk", rows, jnp.asarray(onehot), precision=lax.Precision.HIGHEST)
    tab = tab + jnp.asarray(mask_bias)[None, None, :, None, :]
    tab = tab.reshape(N_HEAD_TILES, HEADS_PER_TILE, WIN_ROWS, GRID_W, WIN_ROWS * GRID_W)
    return tab.transpose(0, 2, 1, 3, 4).reshape(N_HEAD_TILES, WIN_ROWS, HEADS_PER_TILE * GRID_W, WIN_ROWS * GRID_W)


def _out_proj_kernel(a_ref, x_ref, mod_ref, w_ref, o_ref):
    o_ref[0] = x_ref[0] + mod_ref[0, 2:3, :] * _dot(a_ref[0].astype(BF16), w_ref[...])


def _out_proj_call(a, x, mod, w, tm):
    B, T, D = x.shape
    row_spec = pl.BlockSpec((1, tm, D), lambda b, i: (b, i, 0))
    return pl.pallas_call(
        _out_proj_kernel,
        grid=(B, T // tm),
        in_specs=[row_spec, row_spec, pl.BlockSpec((1, 6, D), lambda b, i: (b, 0, 0)), _const_spec((D, D))],
        out_specs=row_spec,
        out_shape=jax.ShapeDtypeStruct((B, T, D), F32),
        compiler_params=_params("parallel", "parallel"),
        name="out_proj",
    )(a, x, mod, w)


def _pad_lora_in(w):
    return jnp.pad(w, ((0, 0), (0, LORA_PAD - w.shape[1])))


def _rwkv_layer_params(i, mu, w_rkv, w_o, w0, w1, w2, a0, a1, a2, g1, g2, k_k, k_a, r_k, ln_w, ln_b):
    rank = w1.shape[-1]
    wl1 = jnp.stack([jnp.concatenate([w1[i, 0], w1[i, 1]], axis=1),
                     jnp.concatenate([a1[i, 0], a1[i, 1]], axis=1),
                     _pad_lora_in(g1[i])]).astype(BF16)
    zeros = jnp.zeros((rank, D_MODEL), F32)
    w2p = jnp.stack([jnp.concatenate([w2[i, 0], zeros]), jnp.concatenate([zeros, w2[i, 1]])]).astype(BF16)
    a2p = jnp.stack([jnp.concatenate([a2[i, 0], zeros]), jnp.concatenate([zeros, a2[i, 1]])]).astype(BF16)
    g2p = jnp.pad(g2[i], ((0, LORA_PAD - g2.shape[1]), (0, 0))).astype(BF16)
    return dict(mu=mu[i], w_rkv=w_rkv[i].astype(BF16), wl1=wl1, w2p=w2p, w0=w0[i], a2p=a2p, a0=a0[i], g2=g2p,
                k_k=k_k[i][None], k_a=k_a[i][None], r_k=r_k[i].reshape(1, D_MODEL),
                ln_w=ln_w[i][None], ln_b=ln_b[i][None], w_o=w_o[i].astype(BF16))


def _segment_matrices():
    e = (np.arange(D_MODEL)[:, None] // HEAD == np.arange(LANES)[None, :]).astype(np.float32)
    return jnp.asarray(e, BF16), jnp.asarray(e.T, BF16)


def kernel(x_prompt, x_sample, state_rwkv, cache_na_k, cache_na_v, c, c_ctx, norm_g, ada_w, ada_b, mlp_w1, mlp_w2, rwkv_mu, rwkv_w_rkv, rwkv_w_o, rwkv_w0, rwkv_w1, rwkv_w2, rwkv_a0, rwkv_a1, rwkv_a2, rwkv_g1, rwkv_g2, rwkv_k_k, rwkv_k_a, rwkv_r_k, rwkv_ln_w, rwkv_ln_b, na_w_qkv, na_w_o, na_q_g, na_k_g, na_rpb):
    n_dec = c.shape[0]
    bp = x_prompt.shape[0]
    cond_rows = 16
    cond = jnp.zeros((cond_rows, D_MODEL), F32).at[:n_dec].set(c).at[n_dec].set(c_ctx)
    mods = _ada_call(cond, ada_w, ada_b)
    mod_lat = mods[:, :n_dec].reshape(DEPTH, n_dec, 6, D_MODEL)
    mod_ctx = jnp.broadcast_to(mods[:, n_dec].reshape(DEPTH, 1, 6, D_MODEL), (DEPTH, bp, 6, D_MODEL))

    rwkv_raw = (rwkv_mu, rwkv_w_rkv, rwkv_w_o, rwkv_w0, rwkv_w1, rwkv_w2, rwkv_a0, rwkv_a1, rwkv_a2,
                rwkv_g1, rwkv_g2, rwkv_k_k, rwkv_k_a, rwkv_r_k, rwkv_ln_w, rwkv_ln_b)
    n_rwkv = rwkv_mu.shape[0]
    n_na = na_w_qkv.shape[0]
    rwkv_p = [_rwkv_layer_params(i, *rwkv_raw) for i in range(n_rwkv)]
    seg = _segment_matrices()
    w1_bf = mlp_w1.astype(BF16)
    w2_bf = mlp_w2.astype(BF16)
    wqkv_bf = na_w_qkv.astype(BF16)
    wo_bf = na_w_o.astype(BF16)
    bias_tabs = [_na_bias_table(na_rpb[i]) for i in range(n_na)]

    def run(x, mod_all, tm, s0_fn, attn_fn):
        states = []
        for l in range(DEPTH):
            i = l // 2
            mod = mod_all[l]
            if l % 2 == 0:
                p = rwkv_p[i]
                r, k, v, gate, lw, a = _rwkv_proj_call(x, mod, norm_g[l, 0][None], p, tm)
                y, bon, s_fin = _rwkv_scan_call(r, k, v, lw, a, s0_fn(i), p, 256, 8)
                states.append(s_fin)
                x = _rwkv_post_call(y, bon, gate, x, mod, p, seg, tm)
            else:
                qkv = _na_qkv_call(x, mod, norm_g[l, 0][None], wqkv_bf[i], tm)
                o = attn_fn(i, qkv)
                x = _out_proj_call(o, x, mod, wo_bf[i], tm)
            x = _mlp_call(x, mod, norm_g[l, 1][None], w1_bf[l], w2_bf[l], tm)
        return x, states

    new_k, new_v = [], []

    def ctx_attn(i, qkv):
        o, k_c, v_c = _na_ctx_call(qkv, na_q_g[i][None], na_k_g[i][None])
        new_k.append(k_c)
        new_v.append(v_c)
        return o

    def lat_attn(i, qkv):
        return _na_lat_call(qkv, cache_na_k[:, i], cache_na_v[:, i], bias_tabs[i], na_q_g[i][None], na_k_g[i][None])

    zero_state = jnp.zeros((bp, 2, HEADS, HEAD, HEAD), F32)
    y_prompt, new_states = run(x_prompt, mod_ctx, 256, lambda i: zero_state, ctx_attn)
    y_sample, _ = run(x_sample, mod_lat, 256, lambda i: state_rwkv[:, i], lat_attn)
    return (y_prompt, y_sample, jnp.stack(new_states, axis=1), jnp.stack(new_k, axis=1), jnp.stack(new_v, axis=1))
```

```python
import functools

import jax
import jax.numpy as jnp
import numpy as np
from jax import lax
from jax.experimental import pallas as pl
from jax.experimental.pallas import tpu as pltpu

F32 = jnp.float32
BF16 = jnp.bfloat16

D_MODEL = 1024
DEPTH = 4
HEADS = 16
HEAD = 64
LANES = 128
HEADS_PER_TILE = LANES // HEAD
N_HEAD_TILES = HEADS // HEADS_PER_TILE
SUBLANES = 8
D_FF = 4 * D_MODEL
LORA_PAD = 128
GRID_W = 64
WIN_ROWS = 8
WIN_COLS = 16
NORM_EPS = 1e-6
GN_EPS = 64e-5
ATTN_SCALE = HEAD ** -0.5
NEG_BIG = -1e30
DECAY_SCALE = float(np.exp(-0.5))
SCAN_CHUNK = 64
VMEM_LIMIT = 56 * 1024 * 1024

NN = ((1,), (0,))
NT = ((1,), (1,))
TN = ((0,), (0,))


def _dot(a, b, dims=NN):
    return lax.dot_general(a, b, (dims, ((), ())), preferred_element_type=F32)


def _split2(x):
    hi = x.astype(BF16)
    lo = (x - hi.astype(F32)).astype(BF16)
    return hi, lo


def _dot3(a, b, dims=NN):
    ah, al = _split2(a)
    bh, bl = _split2(b)
    return _dot(ah, bh, dims) + (_dot(ah, bl, dims) + _dot(al, bh, dims))


def _params(*sem):
    return pltpu.CompilerParams(dimension_semantics=sem, vmem_limit_bytes=VMEM_LIMIT)


def _norm_mod(x, g, shift, scale):
    ms = jnp.mean(x * x, axis=-1, keepdims=True)
    return (x * lax.rsqrt(ms + NORM_EPS) * g) * (1.0 + scale) + shift


def _const_spec(shape):
    nd = len(shape)
    return pl.BlockSpec(shape, lambda *_: (0,) * nd, pipeline_mode=pl.Buffered(1))


def _ada_kernel(c_ref, w_ref, b_ref, o_ref):
    c = c_ref[...]
    s = c * jax.nn.sigmoid(c)
    o_ref[0] = _dot3(s, w_ref[0]) + b_ref[0]


def _ada_call(cond, ada_w, ada_b):
    rows = cond.shape[0]
    tn = 768
    n = 6 * D_MODEL
    return pl.pallas_call(
        _ada_kernel,
        grid=(DEPTH, n // tn),
        in_specs=[
            pl.BlockSpec((rows, D_MODEL), lambda l, j: (0, 0)),
            pl.BlockSpec((1, D_MODEL, tn), lambda l, j: (l, 0, j)),
            pl.BlockSpec((1, 1, tn), lambda l, j: (l, 0, j)),
        ],
        out_specs=pl.BlockSpec((1, rows, tn), lambda l, j: (l, 0, j)),
        out_shape=jax.ShapeDtypeStruct((DEPTH, rows, n), F32),
        compiler_params=_params("parallel", "parallel"),
        name="ada_mod",
    )(cond, ada_w, ada_b.reshape(DEPTH, 1, n))


def _rwkv_proj_kernel(x_ref, xp_ref, xn_ref, mod_ref, g_ref, mu_ref, wrkv_ref, wl1_ref, w2p_ref, w0_ref,
                      a2p_ref, a0_ref, g2_ref, r_ref, k_ref, v_ref, gate_ref, lw_ref, a_ref, *, tm, nt):
    i = pl.program_id(1)
    g = g_ref[...]
    shift = mod_ref[0, 0:1, :]
    scale = mod_ref[0, 1:2, :]
    h = _norm_mod(x_ref[0], g, shift, scale)
    hp = _norm_mod(xp_ref[0], g, shift, scale)[SUBLANES - 1:SUBLANES, :]
    hn = _norm_mod(xn_ref[0], g, shift, scale)[0:1, :]
    hp = jnp.where(i == 0, 0.0, hp)
    hn = jnp.where(i == nt - 1, 0.0, hn)
    row = lax.broadcasted_iota(jnp.int32, (tm, 1), 0)
    prev = jnp.where(row == 0, hp, pltpu.roll(h, 1, 0))
    nxt = jnp.where(row == tm - 1, hn, pltpu.roll(h, tm - 1, 0))
    delta = 0.5 * (prev + nxt) - h

    def mix(j):
        return (h + delta * mu_ref[j:j + 1, :]).astype(BF16)

    r_ref[0] = _dot(mix(0), wrkv_ref[0])
    k_ref[0] = _dot(mix(1), wrkv_ref[1])
    v_ref[0] = _dot(mix(2), wrkv_ref[2])
    lw = jnp.tanh(_dot(mix(3), wl1_ref[0])).astype(BF16)
    la = _dot(mix(4), wl1_ref[1]).astype(BF16)
    lg = jax.nn.sigmoid(_dot(mix(5), wl1_ref[2])).astype(BF16)
    gate_ref[0] = _dot(lg, g2_ref[...])
    for d in range(2):
        z = w0_ref[d:d + 1, :] + _dot(lw, w2p_ref[d])
        lw_ref[d, 0] = -DECAY_SCALE * jax.nn.sigmoid(z)
        a_ref[d, 0] = jax.nn.sigmoid(a0_ref[d:d + 1, :] + _dot(la, a2p_ref[d]))


def _rwkv_proj_call(x, mod, g, p, tm):
    B, T, D = x.shape
    nt = T // tm
    tb = tm // SUBLANES
    row_spec = pl.BlockSpec((1, tm, D), lambda b, i: (b, i, 0))
    dir_spec = pl.BlockSpec((2, 1, tm, D), lambda b, i: (0, b, i, 0))
    out_bt = jax.ShapeDtypeStruct((B, T, D), F32)
    out_dir = jax.ShapeDtypeStruct((2, B, T, D), F32)
    return pl.pallas_call(
        functools.partial(_rwkv_proj_kernel, tm=tm, nt=nt),
        grid=(B, nt),
        in_specs=[
            row_spec,
            pl.BlockSpec((1, SUBLANES, D), lambda b, i: (b, jnp.maximum(i * tb - 1, 0), 0)),
            pl.BlockSpec((1, SUBLANES, D), lambda b, i: (b, jnp.minimum((i + 1) * tb, T // SUBLANES - 1), 0)),
            pl.BlockSpec((1, 6, D), lambda b, i: (b, 0, 0)),
            _const_spec((1, D)),
            _const_spec((6, D)),
            _const_spec((3, D, D)),
            _const_spec((3, D, LORA_PAD)),
            _const_spec((2, LORA_PAD, D)),
            _const_spec((2, D)),
            _const_spec((2, LORA_PAD, D)),
            _const_spec((2, D)),
            _const_spec((LORA_PAD, D)),
        ],
        out_specs=[row_spec, row_spec, row_spec, row_spec, dir_spec, dir_spec],
        out_shape=[out_bt, out_bt, out_bt, out_bt, out_dir, out_dir],
        compiler_params=_params("parallel", "parallel"),
        name="rwkv_proj",
    )(x, x, x, mod, g, p["mu"], p["w_rkv"], p["wl1"], p["w2p"], p["w0"], p["a2p"], p["a0"], p["g2"])


def _scan_kernel(r_ref, k_ref, v_ref, lw_ref, a_ref, s0_ref, kk_ref, ka_ref, rk_ref,
                 y_ref, bon_ref, sf_ref, s_ref, qt_ref, y0_ref, gh_ref, ht_ref, et_ref, *, L, n, nb, hpb):
    d = pl.program_id(1)
    blk = pl.program_id(3)
    TB = n * L

    npair = hpb // HEADS_PER_TILE

    @pl.when(blk == 0)
    def _():
        for p in range(npair):
            s_ref[p] = jnp.concatenate([s0_ref[0, 0, HEADS_PER_TILE * p + hh] for hh in range(HEADS_PER_TILE)], axis=1)

    sign = 1 - 2 * d
    row = lax.broadcasted_iota(jnp.int32, (TB, TB), 0)
    col = lax.broadcasted_iota(jnp.int32, (TB, TB), 1)
    tri = jnp.where(((row // L) == (col // L)) & ((row - col) * sign >= 0), 1.0, 0.0).astype(BF16)
    lrow = lax.broadcasted_iota(jnp.int32, (LANES, LANES), 0)
    lcol = lax.broadcasted_iota(jnp.int32, (LANES, LANES), 1)
    head_ones = jnp.where((lrow // HEAD) == (lcol // HEAD), 1.0, 0.0).astype(BF16)
    wrow = lax.broadcasted_iota(jnp.int32, (L, LANES), 0)
    wcol = lax.broadcasted_iota(jnp.int32, (L, LANES), 1)
    gorder = (wrow - wcol % L) * sign
    strict = gorder > 0
    incl = gorder >= 0
    first = wcol < HEAD
    eye_w = jnp.where(wrow == wcol % HEAD, 1.0, 0.0)

    def seg_sum(x):
        xh, xl = _split2(x)
        tiles = [slice(t * LANES, (t + 1) * LANES) for t in range(npair)]
        return jnp.concatenate([_dot(xh[:, t], head_ones) + _dot(xl[:, t], head_ones) for t in tiles], axis=1)

    def cumsum_rows(x):
        xh, xl = _split2(x)
        return _dot(tri, xh) + _dot(tri, xl)

    r2, k2, v2 = r_ref[0], k_ref[0], v_ref[0]
    lw2, a2 = lw_ref[0, 0], a_ref[0, 0]
    kkr = k2 * kk_ref[...]
    kk = kkr * lax.rsqrt(seg_sum(kkr * kkr) + 1e-12)
    kd = k2 * (1.0 + (a2 - 1.0) * ka_ref[...])
    b2 = kk * a2
    bon_ref[0, 0] = seg_sum(r2 * kd * rk_ref[...]) * v2

    cum = cumsum_rows(lw2)
    tot_rows = [jnp.where(d == 0, cum[(j + 1) * L - 1:(j + 1) * L], cum[j * L:j * L + 1]) for j in range(n)]
    tot = jnp.concatenate([jnp.broadcast_to(x, (L, x.shape[1])) for x in tot_rows], axis=0)
    al2 = -kk * jnp.exp(cum - lw2)
    rh2 = r2 * jnp.exp(cum)
    e_neg = jnp.exp(-cum)
    bc2 = b2 * e_neg
    kc2 = kd * e_neg
    e_tail = jnp.exp(tot - cum)
    be2 = b2 * e_tail
    ke2 = kd * e_tail
    for j in range(n):
        et_ref[j * SUBLANES:(j + 1) * SUBLANES, :] = jnp.broadcast_to(jnp.exp(tot_rows[j]), (SUBLANES, tot.shape[1]))

    chains = [(j, p) for j in range(n) for p in range(npair)]
    nch = range(len(chains))

    def part(x, i):
        j, p = chains[i]
        return x[j * L:(j + 1) * L, p * LANES:(p + 1) * LANES]

    def swap(x):
        return pltpu.roll(x, HEAD, 1)

    def only(x, hh):
        return jnp.where(first, x, 0.0) if hh == 0 else jnp.where(first, 0.0, x)

    def bd(x):
        return jnp.concatenate([only(x, 0), only(x, 1)], axis=0).astype(BF16)

    heads = range(HEADS_PER_TILE)
    zeros_w = jnp.zeros((L, LANES), F32)
    al = [part(al2, i) for i in nch]
    rh = [part(rh2, i) for i in nch]
    v = [part(v2, i) for i in nch]
    lhs = [jnp.concatenate([only(al[i], hh) for hh in heads] + [only(rh[i], hh) for hh in heads], axis=0).astype(BF16)
           for i in nch]
    rhs = [jnp.concatenate([part(bc2, i), part(kc2, i)], axis=0).astype(BF16) for i in nch]
    bek = [jnp.concatenate([part(be2, i), part(ke2, i)], axis=0).astype(BF16) for i in nch]
    gram = [_dot(lhs[i], rhs[i], NT) for i in nch]
    top = [[jnp.where(strict, gram[i][hh * L:(hh + 1) * L], 0.0) for hh in heads] for i in nch]
    bot = [[jnp.where(incl, gram[i][(2 + hh) * L:(3 + hh) * L], 0.0).astype(BF16) for hh in heads]
           for i in nch]
    mab = [jnp.where(first, top[i][0], swap(top[i][1])) for i in nch]
    mak = [jnp.where(first, swap(top[i][0]), top[i][1]) for i in nch]
    w = [_dot(mak[i].astype(BF16), bd(v[i])) for i in nch]
    pw = [mab[i].astype(BF16) for i in nch]
    t = [eye_w + pw[i].astype(F32) for i in nch]
    pw = [_dot(pw[i], bd(pw[i])).astype(BF16) for i in nch]
    span = 4
    while span < L:
        both = [_dot(jnp.concatenate([pw[i], t[i].astype(BF16)], axis=0), bd(pw[i])) for i in nch]
        pw = [both[i][:L].astype(BF16) for i in nch]
        t = [t[i] + both[i][L:] for i in nch]
        span *= 2
    t = [t[i] + _dot(t[i].astype(BF16), bd(pw[i])) for i in nch]
    x0 = [jnp.where(first, al[i], swap(w[i])) for i in nch]
    x1 = [jnp.where(first, swap(al[i]), w[i]) for i in nch]
    xbd = [jnp.concatenate([jnp.concatenate([x0[i], zeros_w], axis=1),
                            jnp.concatenate([zeros_w, x1[i]], axis=1)], axis=0).astype(BF16) for i in nch]
    tx = [_dot(t[i].astype(BF16), xbd[i]) for i in nch]
    vz = [[jnp.where(first, 0.0, swap(v[i])), jnp.where(first, 0.0, v[i])] for i in nch]
    z = [[jnp.concatenate([tx[i][:, hh * LANES:(hh + 1) * LANES], vz[i][hh]], axis=0).astype(BF16) for hh in heads]
         for i in nch]
    qy = [[_dot(bot[i][hh], z[i][hh]) for hh in heads] for i in nch]
    ghs = [_dot(jnp.concatenate(z[i], axis=1), bek[i], TN) for i in nch]
    gh = [[ghs[i][hh * LANES:(hh + 1) * LANES] for hh in heads] for i in nch]
    for i in nch:
        j, p = chains[i]
        qt_ref[p, j * L:(j + 1) * L, :] = rh[i] + jnp.where(first, qy[i][0], swap(qy[i][1]))
        y0_ref[p, j * L:(j + 1) * L, :] = jnp.where(first, swap(qy[i][0]), qy[i][1])
        gh_ref[p, j * LANES:(j + 1) * LANES, :] = jnp.concatenate(
            [only(gh[i][0][:HEAD], 0), only(gh[i][1][:HEAD], 1)], axis=0).astype(BF16)
        ht_ref[p, j * L:(j + 1) * L, :] = jnp.where(first, gh[i][0][HEAD:], gh[i][1][HEAD:])

    def chunk_step(jj, carry):
        cj = jj + d * (n - 1 - 2 * jj)
        r0 = pl.multiple_of(cj * L, L)
        g0 = pl.multiple_of(cj * LANES, LANES)
        e0 = pl.multiple_of(cj * SUBLANES, SUBLANES)
        s = [s_ref[p] for p in range(npair)]
        sbd = [bd(x) for x in s]
        y = [_dot(qt_ref[p, pl.ds(r0, L), :].astype(BF16), sbd[p], NT) for p in range(npair)]
        sg = [_dot(s[p].astype(BF16), gh_ref[p, pl.ds(g0, LANES), :]) for p in range(npair)]
        for p in range(npair):
            y_ref[0, 0, pl.ds(r0, L), p * LANES:(p + 1) * LANES] = y[p] + y0_ref[p, pl.ds(r0, L), :]
            decay = et_ref[pl.ds(e0, SUBLANES), p * LANES:(p + 1) * LANES][0:1, :]
            s_ref[p] = s[p] * decay + sg[p] + ht_ref[p, pl.ds(r0, L), :]
        return carry

    lax.fori_loop(0, n, chunk_step, 0, unroll=True)

    @pl.when(blk == nb - 1)
    def _():
        for p in range(npair):
            for hh in range(HEADS_PER_TILE):
                sf_ref[0, 0, HEADS_PER_TILE * p + hh] = s_ref[p][:, hh * HEAD:(hh + 1) * HEAD]


def _rwkv_scan_call(r, k, v, lw, a, s0, p, tb, hpb):
    B, T, D = r.shape
    L = SCAN_CHUNK
    n = tb // L
    nb = T // tb
    wl = hpb * HEAD
    npair = hpb // HEADS_PER_TILE

    def bidx(d, c):
        return c + d * (nb - 1 - 2 * c)

    row_spec = pl.BlockSpec((1, tb, wl), lambda b, d, h, c: (b, bidx(d, c), h))
    dir_spec = pl.BlockSpec((1, 1, tb, wl), lambda b, d, h, c: (d, b, bidx(d, c), h))
    st_spec = pl.BlockSpec((1, 1, hpb, HEAD, HEAD), lambda b, d, h, c: (b, d, h, 0, 0))
    vec_spec = pl.BlockSpec((1, wl), lambda b, d, h, c: (0, h))
    return pl.pallas_call(
        functools.partial(_scan_kernel, L=L, n=n, nb=nb, hpb=hpb),
        grid=(B, 2, HEADS // hpb, nb),
        in_specs=[row_spec, row_spec, row_spec, dir_spec, dir_spec, st_spec, vec_spec, vec_spec, vec_spec],
        out_specs=[dir_spec, dir_spec, st_spec],
        out_shape=[jax.ShapeDtypeStruct((2, B, T, D), F32), jax.ShapeDtypeStruct((2, B, T, D), F32),
                   jax.ShapeDtypeStruct((B, 2, HEADS, HEAD, HEAD), F32)],
        scratch_shapes=[pltpu.VMEM((npair, HEAD, LANES), F32),
                        pltpu.VMEM((npair, tb, LANES), F32),
                        pltpu.VMEM((npair, tb, LANES), F32),
                        pltpu.VMEM((npair, n * LANES, LANES), BF16),
                        pltpu.VMEM((npair, tb, LANES), F32),
                        pltpu.VMEM((n * SUBLANES, wl), F32)],
        compiler_params=_params("parallel", "parallel", "parallel", "arbitrary"),
        name="rwkv_scan",
    )(r, k, v, lw, a, s0, p["k_k"], p["k_a"], p["r_k"])


def _seg_mean(x, e_ref, et_ref):
    xh, xl = _split2(x)
    s = (_dot(xh, e_ref[...]) + _dot(xl, e_ref[...])) * (1.0 / HEAD)
    sh, slo = _split2(s)
    return _dot(sh, et_ref[...]) + _dot(slo, et_ref[...])


def _rwkv_post_kernel(y_ref, bon_ref, gate_ref, x_ref, mod_ref, lnw_ref, lnb_ref, e_ref, et_ref, wo_ref, o_ref):
    y = y_ref[0, 0] + y_ref[1, 0]
    yc = y - _seg_mean(y, e_ref, et_ref)
    var = _seg_mean(yc * yc, e_ref, et_ref)
    yn = yc * lax.rsqrt(var + GN_EPS) * lnw_ref[...] + lnb_ref[...]
    o = (yn + (bon_ref[0, 0] + bon_ref[1, 0])) * gate_ref[0]
    o_ref[0] = x_ref[0] + mod_ref[0, 2:3, :] * _dot(o.astype(BF16), wo_ref[...])


def _rwkv_post_call(y, bon, gate, x, mod, p, seg, tm):
    B, T, D = x.shape
    row_spec = pl.BlockSpec((1, tm, D), lambda b, i: (b, i, 0))
    dir_spec = pl.BlockSpec((2, 1, tm, D), lambda b, i: (0, b, i, 0))
    return pl.pallas_call(
        _rwkv_post_kernel,
        grid=(B, T // tm),
        in_specs=[dir_spec, dir_spec, row_spec, row_spec, pl.BlockSpec((1, 6, D), lambda b, i: (b, 0, 0)),
                  _const_spec((1, D)), _const_spec((1, D)), _const_spec((D, LANES)), _const_spec((LANES, D)),
                  _const_spec((D, D))],
        out_specs=row_spec,
        out_shape=jax.ShapeDtypeStruct((B, T, D), F32),
        compiler_params=_params("parallel", "parallel"),
        name="rwkv_post",
    )(y, bon, gate, x, mod, p["ln_w"], p["ln_b"], seg[0], seg[1], p["w_o"])


def _mlp_kernel(x_ref, mod_ref, g_ref, w1_ref, w2_ref, o_ref, *, nchunk):
    x = x_ref[0]
    h = _norm_mod(x, g_ref[...], mod_ref[0, 3:4, :], mod_ref[0, 4:5, :]).astype(BF16)
    fc = D_FF // nchunk
    acc = jnp.zeros(x.shape, F32)
    for j in range(nchunk):
        hid = jnp.maximum(_dot(h, w1_ref[:, j * fc:(j + 1) * fc]), 0.0)
        acc = acc + _dot((hid * hid).astype(BF16), w2_ref[j * fc:(j + 1) * fc, :])
    o_ref[0] = x + mod_ref[0, 5:6, :] * acc


def _mlp_call(x, mod, g, w1, w2, tm):
    B, T, D = x.shape
    row_spec = pl.BlockSpec((1, tm, D), lambda b, i: (b, i, 0))
    return pl.pallas_call(
        functools.partial(_mlp_kernel, nchunk=4),
        grid=(B, T // tm),
        in_specs=[row_spec, pl.BlockSpec((1, 6, D), lambda b, i: (b, 0, 0)), _const_spec((1, D)),
                  _const_spec((D, D_FF)), _const_spec((D_FF, D))],
        out_specs=row_spec,
        out_shape=jax.ShapeDtypeStruct((B, T, D), F32),
        compiler_params=_params("parallel", "parallel"),
        name="mlp",
    )(x, mod, g, w1, w2)


def _na_qkv_kernel(x_ref, mod_ref, g_ref, w_ref, o_ref):
    h = _norm_mod(x_ref[0], g_ref[...], mod_ref[0, 0:1, :], mod_ref[0, 1:2, :]).astype(BF16)
    o_ref[0] = _dot(h, w_ref[...])


def _na_qkv_call(x, mod, g, w, tm):
    B, T, D = x.shape
    return pl.pallas_call(
        _na_qkv_kernel,
        grid=(B, T // tm),
        in_specs=[pl.BlockSpec((1, tm, D), lambda b, i: (b, i, 0)), pl.BlockSpec((1, 6, D), lambda b, i: (b, 0, 0)),
                  _const_spec((1, D)), _const_spec((D, 3 * D))],
        out_specs=pl.BlockSpec((1, tm, 3 * D), lambda b, i: (b, i, 0)),
        out_shape=jax.ShapeDtypeStruct((B, T, 3 * D), F32),
        compiler_params=_params("parallel", "parallel"),
        name="na_qkv",
    )(x, mod, g, w)


def _head_norm(x, g):
    return x * lax.rsqrt(jnp.mean(x * x, axis=-1, keepdims=True) + NORM_EPS) * g


def _na_ctx_kernel(q_ref, k_ref, v_ref, qg_ref, kg_ref, o_ref, kc_ref, vc_ref):
    q2, k2, v2 = q_ref[0], k_ref[0], v_ref[0]
    for hh in range(HEADS_PER_TILE):
        sl = slice(hh * HEAD, (hh + 1) * HEAD)
        q = _head_norm(q2[:, sl], qg_ref[...])
        k = _head_norm(k2[:, sl], kg_ref[...])
        v = v2[:, sl]
        kc_ref[0, hh] = k
        vc_ref[0, hh] = v
        s = _dot(q.astype(BF16), k.astype(BF16), NT) * ATTN_SCALE
        e = jnp.exp(s - jnp.max(s, axis=-1, keepdims=True))
        o = _dot(e.astype(BF16), v.astype(BF16)) / jnp.sum(e, axis=-1, keepdims=True)
        o_ref[0, :, sl] = o


def _qkv_specs(T):
    return [pl.BlockSpec((1, T, LANES), lambda b, h, part=part: (b, 0, part * N_HEAD_TILES + h)) for part in range(3)]


def _na_ctx_call(qkv, qg, kg):
    B, T, _ = qkv.shape
    cache_spec = pl.BlockSpec((1, HEADS_PER_TILE, T, HEAD), lambda b, h: (b, h, 0, 0))
    cache_shape = jax.ShapeDtypeStruct((B, HEADS, T, HEAD), F32)
    return pl.pallas_call(
        _na_ctx_kernel,
        grid=(B, N_HEAD_TILES),
        in_specs=_qkv_specs(T) + [_const_spec((1, HEAD)), _const_spec((1, HEAD))],
        out_specs=[pl.BlockSpec((1, T, LANES), lambda b, h: (b, 0, h)), cache_spec, cache_spec],
        out_shape=[jax.ShapeDtypeStruct((B, T, D_MODEL), F32), cache_shape, cache_shape],
        compiler_params=_params("parallel", "parallel"),
        name="na_ctx",
    )(qkv, qkv, qkv, qg, kg)


def _na_lat_kernel(q_ref, k_ref, v_ref, kc_ref, vc_ref, bias_ref, qg_ref, kg_ref, o_ref, qn_ref, kn_ref, vn_ref,
                   *, rows, rg):
    lrow = lax.broadcasted_iota(jnp.int32, (LANES, LANES), 0)
    lcol = lax.broadcasted_iota(jnp.int32, (LANES, LANES), 1)
    head_ones = jnp.where((lrow // HEAD) == (lcol // HEAD), 1.0, 0.0).astype(BF16)

    def tile_head_norm(x, g):
        xh, xl = _split2(x * x)
        ms = (_dot(xh, head_ones) + _dot(xl, head_ones)) * (1.0 / HEAD)
        return x * lax.rsqrt(ms + NORM_EPS) * g

    qn_ref[...] = (tile_head_norm(q_ref[0], qg_ref[...]) * ATTN_SCALE).astype(BF16)
    kn_ref[...] = tile_head_norm(k_ref[0], kg_ref[...]).astype(BF16)
    vn_ref[...] = v_ref[0].astype(BF16)
    win = WIN_ROWS * GRID_W
    kc2 = jnp.concatenate([kc_ref[0, hh] for hh in range(HEADS_PER_TILE)], axis=1).astype(BF16)
    vc2 = jnp.concatenate([vc_ref[0, hh] for hh in range(HEADS_PER_TILE)], axis=1).astype(BF16)
    first_head = lax.broadcasted_iota(jnp.int32, (GRID_W, LANES), 1) < HEAD
    nrg = range(rg)

    def row_group(gi, carry):
        q0, k0, var = [], [], []
        for rr in nrg:
            r = gi * rg + rr
            rs = jnp.clip(r - WIN_ROWS // 2, 0, rows - WIN_ROWS)
            var.append(rs - r + WIN_ROWS - 1)
            q0.append(pl.multiple_of(r * GRID_W, GRID_W))
            k0.append(pl.multiple_of(rs * GRID_W, GRID_W))
        qt = [qn_ref[pl.ds(q0[rr], GRID_W), :] for rr in nrg]
        zero = jnp.zeros((GRID_W, LANES), BF16)
        q = [jnp.concatenate([jnp.where(first_head, qt[rr], zero), jnp.where(first_head, zero, qt[rr])], axis=0)
             for rr in nrg]
        s_w = [_dot(q[rr], kn_ref[pl.ds(k0[rr], win), :], NT) + bias_ref[0, var[rr]] for rr in nrg]
        s_c = [_dot(q[rr], kc2, NT) for rr in nrg]
        m = [jnp.maximum(jnp.max(s_w[rr], axis=-1, keepdims=True), jnp.max(s_c[rr], axis=-1, keepdims=True))
             for rr in nrg]
        e_w = [jnp.exp(s_w[rr] - m[rr]) for rr in nrg]
        e_c = [jnp.exp(s_c[rr] - m[rr]) for rr in nrg]
        den = [jnp.sum(e_w[rr], axis=-1, keepdims=True) + jnp.sum(e_c[rr], axis=-1, keepdims=True) for rr in nrg]
        o = [(_dot(e_w[rr].astype(BF16), vn_ref[pl.ds(k0[rr], win), :]) + _dot(e_c[rr].astype(BF16), vc2)) / den[rr]
             for rr in nrg]
        for rr in nrg:
            o_ref[0, pl.ds(q0[rr], GRID_W), :] = jnp.where(first_head, o[rr][:GRID_W], o[rr][GRID_W:])
        return carry

    lax.fori_loop(0, rows // rg, row_group, 0)


def _na_lat_call(qkv, k_ctx, v_ctx, bias, qg, kg):
    B, T, _ = qkv.shape
    P = k_ctx.shape[2]
    rows = T // GRID_W
    ctx_spec = pl.BlockSpec((1, HEADS_PER_TILE, P, HEAD), lambda b, h: (b, h, 0, 0))
    return pl.pallas_call(
        functools.partial(_na_lat_kernel, rows=rows, rg=8),
        grid=(B, N_HEAD_TILES),
        in_specs=_qkv_specs(T) + [
            ctx_spec, ctx_spec,
            pl.BlockSpec((1, WIN_ROWS, HEADS_PER_TILE * GRID_W, WIN_ROWS * GRID_W), lambda b, h: (h, 0, 0, 0)),
            _const_spec((1, LANES)), _const_spec((1, LANES))],
        out_specs=pl.BlockSpec((1, T, LANES), lambda b, h: (b, 0, h)),
        out_shape=jax.ShapeDtypeStruct((B, T, D_MODEL), F32),
        scratch_shapes=[pltpu.VMEM((T, LANES), BF16)] * 3,
        compiler_params=_params("parallel", "parallel"),
        name="na_lat",
    )(qkv, qkv, qkv, k_ctx, v_ctx, bias, jnp.tile(qg, (1, HEADS_PER_TILE)), jnp.tile(kg, (1, HEADS_PER_TILE)))


def _na_bias_table(rpb):
    n_col = 2 * WIN_COLS - 1
    qc = np.arange(GRID_W)[:, None]
    kc = np.arange(GRID_W)[None, :]
    col = np.clip(kc - qc + WIN_COLS - 1, 0, n_col - 1)
    ws = np.clip(qc - WIN_COLS // 2, 0, GRID_W - WIN_COLS)
    valid = (kc >= ws) & (kc < ws + WIN_COLS)
    onehot = ((col[None] == np.arange(n_col)[:, None, None]) & valid[None]).astype(np.float32)
    mask_bias = np.where(valid, 0.0, NEG_BIG).astype(np.float32)
    rows = jnp.stack([rpb[:, al:al + WIN_ROWS, :] for al in range(WIN_ROWS)], axis=1)
    tab = jnp.einsum("hajc,cqk->haqjk", rows, jnp.asarray(onehot), precision=lax.Precision.HIGHEST)
    tab = tab + jnp.asarray(mask_bias)[None, None, :, None, :]
    tab = tab.reshape(N_HEAD_TILES, HEADS_PER_TILE, WIN_ROWS, GRID_W, WIN_ROWS * GRID_W)
    return tab.transpose(0, 2, 1, 3, 4).reshape(N_HEAD_TILES, WIN_ROWS, HEADS_PER_TILE * GRID_W, WIN_ROWS * GRID_W)


def _out_proj_kernel(a_ref, x_ref, mod_ref, w_ref, o_ref):
    o_ref[0] = x_ref[0] + mod_ref[0, 2:3, :] * _dot(a_ref[0].astype(BF16), w_ref[...])


def _out_proj_call(a, x, mod, w, tm):
    B, T, D = x.shape
    row_spec = pl.BlockSpec((1, tm, D), lambda b, i: (b, i, 0))
    return pl.pallas_call(
        _out_proj_kernel,
        grid=(B, T // tm),
        in_specs=[row_spec, row_spec, pl.BlockSpec((1, 6, D), lambda b, i: (b, 0, 0)), _const_spec((D, D))],
        out_specs=row_spec,
        out_shape=jax.ShapeDtypeStruct((B, T, D), F32),
        compiler_params=_params("parallel", "parallel"),
        name="out_proj",
    )(a, x, mod, w)


def _pad_lora_in(w):
    return jnp.pad(w, ((0, 0), (0, LORA_PAD - w.shape[1])))


def _rwkv_layer_params(i, mu, w_rkv, w_o, w0, w1, w2, a0, a1, a2, g1, g2, k_k, k_a, r_k, ln_w, ln_b):
    rank = w1.shape[-1]
    wl1 = jnp.stack([jnp.concatenate([w1[i, 0], w1[i, 1]], axis=1),
                     jnp.concatenate([a1[i, 0], a1[i, 1]], axis=1),
                     _pad_lora_in(g1[i])]).astype(BF16)
    zeros = jnp.zeros((rank, D_MODEL), F32)
    w2p = jnp.stack([jnp.concatenate([w2[i, 0], zeros]), jnp.concatenate([zeros, w2[i, 1]])]).astype(BF16)
    a2p = jnp.stack([jnp.concatenate([a2[i, 0], zeros]), jnp.concatenate([zeros, a2[i, 1]])]).astype(BF16)
    g2p = jnp.pad(g2[i], ((0, LORA_PAD - g2.shape[1]), (0, 0))).astype(BF16)
    return dict(mu=mu[i], w_rkv=w_rkv[i].astype(BF16), wl1=wl1, w2p=w2p, w0=w0[i], a2p=a2p, a0=a0[i], g2=g2p,
                k_k=k_k[i][None], k_a=k_a[i][None], r_k=r_k[i].reshape(1, D_MODEL),
                ln_w=ln_w[i][None], ln_b=ln_b[i][None], w_o=w_o[i].astype(BF16))


def _segment_matrices():
    e = (np.arange(D_MODEL)[:, None] // HEAD == np.arange(LANES)[None, :]).astype(np.float32)
    return jnp.asarray(e, BF16), jnp.asarray(e.T, BF16)


def kernel(x_prompt, x_sample, state_rwkv, cache_na_k, cache_na_v, c, c_ctx, norm_g, ada_w, ada_b, mlp_w1, mlp_w2, rwkv_mu, rwkv_w_rkv, rwkv_w_o, rwkv_w0, rwkv_w1, rwkv_w2, rwkv_a0, rwkv_a1, rwkv_a2, rwkv_g1, rwkv_g2, rwkv_k_k, rwkv_k_a, rwkv_r_k, rwkv_ln_w, rwkv_ln_b, na_w_qkv, na_w_o, na_q_g, na_k_g, na_rpb):
    n_dec = c.shape[0]
    bp = x_prompt.shape[0]
    cond_rows = 16
    cond = jnp.zeros((cond_rows, D_MODEL), F32).at[:n_dec].set(c).at[n_dec].set(c_ctx)
    mods = _ada_call(cond, ada_w, ada_b)
    mod_lat = mods[:, :n_dec].reshape(DEPTH, n_dec, 6, D_MODEL)
    mod_ctx = jnp.broadcast_to(mods[:, n_dec].reshape(DEPTH, 1, 6, D_MODEL), (DEPTH, bp, 6, D_MODEL))

    rwkv_raw = (rwkv_mu, rwkv_w_rkv, rwkv_w_o, rwkv_w0, rwkv_w1, rwkv_w2, rwkv_a0, rwkv_a1, rwkv_a2,
                rwkv_g1, rwkv_g2, rwkv_k_k, rwkv_k_a, rwkv_r_k, rwkv_ln_w, rwkv_ln_b)
    n_rwkv = rwkv_mu.shape[0]
    n_na = na_w_qkv.shape[0]
    rwkv_p = [_rwkv_layer_params(i, *rwkv_raw) for i in range(n_rwkv)]
    seg = _segment_matrices()
    w1_bf = mlp_w1.astype(BF16)
    w2_bf = mlp_w2.astype(BF16)
    wqkv_bf = na_w_qkv.astype(BF16)
    wo_bf = na_w_o.astype(BF16)
    bias_tabs = [_na_bias_table(na_rpb[i]) for i in range(n_na)]

    def run(x, mod_all, tm, s0_fn, attn_fn):
        states = []
        for l in range(DEPTH):
            i = l // 2
            mod = mod_all[l]
            if l % 2 == 0:
                p = rwkv_p[i]
                r, k, v, gate, lw, a = _rwkv_proj_call(x, mod, norm_g[l, 0][None], p, tm)
                y, bon, s_fin = _rwkv_scan_call(r, k, v, lw, a, s0_fn(i), p, 256, 8)
                states.append(s_fin)
                x = _rwkv_post_call(y, bon, gate, x, mod, p, seg, tm)
            else:
                qkv = _na_qkv_call(x, mod, norm_g[l, 0][None], wqkv_bf[i], tm)
                o = attn_fn(i, qkv)
                x = _out_proj_call(o, x, mod, wo_bf[i], tm)
            x = _mlp_call(x, mod, norm_g[l, 1][None], w1_bf[l], w2_bf[l], tm)
        return x, states

    new_k, new_v = [], []

    def ctx_attn(i, qkv):
        o, k_c, v_c = _na_ctx_call(qkv, na_q_g[i][None], na_k_g[i][None])
        new_k.append(k_c)
        new_v.append(v_c)
        return o

    def lat_attn(i, qkv):
        return _na_lat_call(qkv, cache_na_k[:, i], cache_na_v[:, i], bias_tabs[i], na_q_g[i][None], na_k_g[i][None])

    zero_state = jnp.zeros((bp, 2, HEADS, HEAD, HEAD), F32)
    y_prompt, new_states = run(x_prompt, mod_ctx, 256, lambda i: zero_state, ctx_attn)
    y_sample, _ = run(x_sample, mod_lat, 512, lambda i: state_rwkv[:, i], lat_attn)
    return (y_prompt, y_sample, jnp.stack(new_states, axis=1), jnp.stack(new_k, axis=1), jnp.stack(new_v, axis=1))
```

```python
import functools

import jax
import jax.numpy as jnp
import numpy as np
from jax import lax
from jax.experimental import pallas as pl
from jax.experimental.pallas import tpu as pltpu

F32 = jnp.float32
BF16 = jnp.bfloat16

D_MODEL = 1024
DEPTH = 4
HEADS = 16
HEAD = 64
LANES = 128
HEADS_PER_TILE = LANES // HEAD
N_HEAD_TILES = HEADS // HEADS_PER_TILE
SUBLANES = 8
D_FF = 4 * D_MODEL
LORA_PAD = 128
GRID_W = 64
WIN_ROWS = 8
WIN_COLS = 16
NORM_EPS = 1e-6
GN_EPS = 64e-5
ATTN_SCALE = HEAD ** -0.5
NEG_BIG = -1e30
DECAY_SCALE = float(np.exp(-0.5))
SCAN_CHUNK = 64
VMEM_LIMIT = 56 * 1024 * 1024

NN = ((1,), (0,))
NT = ((1,), (1,))
TN = ((0,), (0,))


def _dot(a, b, dims=NN):
    return lax.dot_general(a, b, (dims, ((), ())), preferred_element_type=F32)


def _split2(x):
    hi = x.astype(BF16)
    lo = (x - hi.astype(F32)).astype(BF16)
    return hi, lo


def _dot3(a, b, dims=NN):
    ah, al = _split2(a)
    bh, bl = _split2(b)
    return _dot(ah, bh, dims) + (_dot(ah, bl, dims) + _dot(al, bh, dims))


def _params(*sem):
    return pltpu.CompilerParams(dimension_semantics=sem, vmem_limit_bytes=VMEM_LIMIT)


def _norm_mod(x, g, shift, scale):
    ms = jnp.mean(x * x, axis=-1, keepdims=True)
    return (x * lax.rsqrt(ms + NORM_EPS) * g) * (1.0 + scale) + shift


def _const_spec(shape):
    nd = len(shape)
    return pl.BlockSpec(shape, lambda *_: (0,) * nd, pipeline_mode=pl.Buffered(1))


def _ada_kernel(c_ref, w_ref, b_ref, o_ref):
    c = c_ref[...]
    s = c * jax.nn.sigmoid(c)
    o_ref[0] = _dot3(s, w_ref[0]) + b_ref[0]


def _ada_call(cond, ada_w, ada_b):
    rows = cond.shape[0]
    tn = 768
    n = 6 * D_MODEL
    return pl.pallas_call(
        _ada_kernel,
        grid=(DEPTH, n // tn),
        in_specs=[
            pl.BlockSpec((rows, D_MODEL), lambda l, j: (0, 0)),
            pl.BlockSpec((1, D_MODEL, tn), lambda l, j: (l, 0, j)),
            pl.BlockSpec((1, 1, tn), lambda l, j: (l, 0, j)),
        ],
        out_specs=pl.BlockSpec((1, rows, tn), lambda l, j: (l, 0, j)),
        out_shape=jax.ShapeDtypeStruct((DEPTH, rows, n), F32),
        compiler_params=_params("parallel", "parallel"),
        name="ada_mod",
    )(cond, ada_w, ada_b.reshape(DEPTH, 1, n))


def _rwkv_proj_kernel(x_ref, xp_ref, xn_ref, mod_ref, g_ref, mu_ref, wrkv_ref, wl1_ref, w2p_ref, w0_ref,
                      a2p_ref, a0_ref, g2_ref, r_ref, k_ref, v_ref, gate_ref, lw_ref, a_ref, *, tm, nt):
    i = pl.program_id(1)
    g = g_ref[...]
    shift = mod_ref[0, 0:1, :]
    scale = mod_ref[0, 1:2, :]
    h = _norm_mod(x_ref[0], g, shift, scale)
    hp = _norm_mod(xp_ref[0], g, shift, scale)[SUBLANES - 1:SUBLANES, :]
    hn = _norm_mod(xn_ref[0], g, shift, scale)[0:1, :]
    hp = jnp.where(i == 0, 0.0, hp)
    hn = jnp.where(i == nt - 1, 0.0, hn)
    row = lax.broadcasted_iota(jnp.int32, (tm, 1), 0)
    prev = jnp.where(row == 0, hp, pltpu.roll(h, 1, 0))
    nxt = jnp.where(row == tm - 1, hn, pltpu.roll(h, tm - 1, 0))
    delta = 0.5 * (prev + nxt) - h

    def mix(j):
        return (h + delta * mu_ref[j:j + 1, :]).astype(BF16)

    r_ref[0] = _dot(mix(0), wrkv_ref[0])
    k_ref[0] = _dot(mix(1), wrkv_ref[1])
    v_ref[0] = _dot(mix(2), wrkv_ref[2])
    lw = jnp.tanh(_dot(mix(3), wl1_ref[0])).astype(BF16)
    la = _dot(mix(4), wl1_ref[1]).astype(BF16)
    lg = jax.nn.sigmoid(_dot(mix(5), wl1_ref[2])).astype(BF16)
    gate_ref[0] = _dot(lg, g2_ref[...])
    for d in range(2):
        z = w0_ref[d:d + 1, :] + _dot(lw, w2p_ref[d])
        lw_ref[d, 0] = -DECAY_SCALE * jax.nn.sigmoid(z)
        a_ref[d, 0] = jax.nn.sigmoid(a0_ref[d:d + 1, :] + _dot(la, a2p_ref[d]))


def _rwkv_proj_call(x, mod, g, p, tm):
    B, T, D = x.shape
    nt = T // tm
    tb = tm // SUBLANES
    row_spec = pl.BlockSpec((1, tm, D), lambda b, i: (b, i, 0))
    dir_spec = pl.BlockSpec((2, 1, tm, D), lambda b, i: (0, b, i, 0))
    out_bt = jax.ShapeDtypeStruct((B, T, D), F32)
    out_dir = jax.ShapeDtypeStruct((2, B, T, D), F32)
    return pl.pallas_call(
        functools.partial(_rwkv_proj_kernel, tm=tm, nt=nt),
        grid=(B, nt),
        in_specs=[
            row_spec,
            pl.BlockSpec((1, SUBLANES, D), lambda b, i: (b, jnp.maximum(i * tb - 1, 0), 0)),
            pl.BlockSpec((1, SUBLANES, D), lambda b, i: (b, jnp.minimum((i + 1) * tb, T // SUBLANES - 1), 0)),
            pl.BlockSpec((1, 6, D), lambda b, i: (b, 0, 0)),
            _const_spec((1, D)),
            _const_spec((6, D)),
            _const_spec((3, D, D)),
            _const_spec((3, D, LORA_PAD)),
            _const_spec((2, LORA_PAD, D)),
            _const_spec((2, D)),
            _const_spec((2, LORA_PAD, D)),
            _const_spec((2, D)),
            _const_spec((LORA_PAD, D)),
        ],
        out_specs=[row_spec, row_spec, row_spec, row_spec, dir_spec, dir_spec],
        out_shape=[out_bt, out_bt, out_bt, out_bt, out_dir, out_dir],
        compiler_params=_params("parallel", "parallel"),
        name="rwkv_proj",
    )(x, x, x, mod, g, p["mu"], p["w_rkv"], p["wl1"], p["w2p"], p["w0"], p["a2p"], p["a0"], p["g2"])


def _scan_kernel(r_ref, k_ref, v_ref, lw_ref, a_ref, s0_ref, kk_ref, ka_ref, rk_ref,
                 y_ref, bon_ref, sf_ref, s_ref, qt_ref, y0_ref, gh_ref, ht_ref, et_ref, *, L, n, nb, hpb):
    d = pl.program_id(1)
    blk = pl.program_id(3)
    TB = n * L

    npair = hpb // HEADS_PER_TILE

    @pl.when(blk == 0)
    def _():
        for p in range(npair):
            s_ref[p] = jnp.concatenate([s0_ref[0, 0, HEADS_PER_TILE * p + hh] for hh in range(HEADS_PER_TILE)], axis=1)

    sign = 1 - 2 * d
    row = lax.broadcasted_iota(jnp.int32, (TB, TB), 0)
    col = lax.broadcasted_iota(jnp.int32, (TB, TB), 1)
    tri = jnp.where(((row // L) == (col // L)) & ((row - col) * sign >= 0), 1.0, 0.0).astype(BF16)
    lrow = lax.broadcasted_iota(jnp.int32, (LANES, LANES), 0)
    lcol = lax.broadcasted_iota(jnp.int32, (LANES, LANES), 1)
    head_ones = jnp.where((lrow // HEAD) == (lcol // HEAD), 1.0, 0.0).astype(BF16)
    wrow = lax.broadcasted_iota(jnp.int32, (L, LANES), 0)
    wcol = lax.broadcasted_iota(jnp.int32, (L, LANES), 1)
    gorder = (wrow - wcol % L) * sign
    strict = gorder > 0
    incl = gorder >= 0
    first = wcol < HEAD
    eye_w = jnp.where(wrow == wcol % HEAD, 1.0, 0.0)

    def seg_sum(x):
        xh, xl = _split2(x)
        tiles = [slice(t * LANES, (t + 1) * LANES) for t in range(npair)]
        return jnp.concatenate([_dot(xh[:, t], head_ones) + _dot(xl[:, t], head_ones) for t in tiles], axis=1)

    def cumsum_rows(x):
        xh, xl = _split2(x)
        return _dot(tri, xh) + _dot(tri, xl)

    r2, k2, v2 = r_ref[0], k_ref[0], v_ref[0]
    lw2, a2 = lw_ref[0, 0], a_ref[0, 0]
    kkr = k2 * kk_ref[...]
    kk = kkr * lax.rsqrt(seg_sum(kkr * kkr) + 1e-12)
    kd = k2 * (1.0 + (a2 - 1.0) * ka_ref[...])
    b2 = kk * a2
    bon_ref[0, 0] = seg_sum(r2 * kd * rk_ref[...]) * v2

    cum = cumsum_rows(lw2)
    tot_rows = [jnp.where(d == 0, cum[(j + 1) * L - 1:(j + 1) * L], cum[j * L:j * L + 1]) for j in range(n)]
    tot = jnp.concatenate([jnp.broadcast_to(x, (L, x.shape[1])) for x in tot_rows], axis=0)
    al2 = -kk * jnp.exp(cum - lw2)
    rh2 = r2 * jnp.exp(cum)
    e_neg = jnp.exp(-cum)
    bc2 = b2 * e_neg
    kc2 = kd * e_neg
    e_tail = jnp.exp(tot - cum)
    be2 = b2 * e_tail
    ke2 = kd * e_tail
    for j in range(n):
        et_ref[j * SUBLANES:(j + 1) * SUBLANES, :] = jnp.broadcast_to(jnp.exp(tot_rows[j]), (SUBLANES, tot.shape[1]))

    chains = [(j, p) for j in range(n) for p in range(npair)]
    nch = range(len(chains))

    def part(x, i):
        j, p = chains[i]
        return x[j * L:(j + 1) * L, p * LANES:(p + 1) * LANES]

    def swap(x):
        return pltpu.roll(x, HEAD, 1)

    def only(x, hh):
        return jnp.where(first, x, 0.0) if hh == 0 else jnp.where(first, 0.0, x)

    def bd(x):
        return jnp.concatenate([only(x, 0), only(x, 1)], axis=0).astype(BF16)

    heads = range(HEADS_PER_TILE)
    zeros_w = jnp.zeros((L, LANES), F32)
    al = [part(al2, i) for i in nch]
    rh = [part(rh2, i) for i in nch]
    v = [part(v2, i) for i in nch]
    lhs = [jnp.concatenate([only(al[i], hh) for hh in heads] + [only(rh[i], hh) for hh in heads], axis=0).astype(BF16)
           for i in nch]
    rhs = [jnp.concatenate([part(bc2, i), part(kc2, i)], axis=0).astype(BF16) for i in nch]
    bek = [jnp.concatenate([part(be2, i), part(ke2, i)], axis=0).astype(BF16) for i in nch]
    gram = [_dot(lhs[i], rhs[i], NT) for i in nch]
    top = [[jnp.where(strict, gram[i][hh * L:(hh + 1) * L], 0.0) for hh in heads] for i in nch]
    bot = [[jnp.where(incl, gram[i][(2 + hh) * L:(3 + hh) * L], 0.0).astype(BF16) for hh in heads]
           for i in nch]
    mab = [jnp.where(first, top[i][0], swap(top[i][1])) for i in nch]
    mak = [jnp.where(first, swap(top[i][0]), top[i][1]) for i in nch]
    w = [_dot(mak[i].astype(BF16), bd(v[i])) for i in nch]
    pw = [mab[i].astype(BF16) for i in nch]
    t = [eye_w + pw[i].astype(F32) for i in nch]
    pw = [_dot(pw[i], bd(pw[i])).astype(BF16) for i in nch]
    span = 4
    while span < L:
        both = [_dot(jnp.concatenate([pw[i], t[i].astype(BF16)], axis=0), bd(pw[i])) for i in nch]
        pw = [both[i][:L].astype(BF16) for i in nch]
        t = [t[i] + both[i][L:] for i in nch]
        span *= 2
    t = [t[i] + _dot(t[i].astype(BF16), bd(pw[i])) for i in nch]
    x0 = [jnp.where(first, al[i], swap(w[i])) for i in nch]
    x1 = [jnp.where(first, swap(al[i]), w[i]) for i in nch]
    xbd = [jnp.concatenate([jnp.concatenate([x0[i], zeros_w], axis=1),
                            jnp.concatenate([zeros_w, x1[i]], axis=1)], axis=0).astype(BF16) for i in nch]
    tx = [_dot(t[i].astype(BF16), xbd[i]) for i in nch]
    vz = [[jnp.where(first, 0.0, swap(v[i])), jnp.where(first, 0.0, v[i])] for i in nch]
    z = [[jnp.concatenate([tx[i][:, hh * LANES:(hh + 1) * LANES], vz[i][hh]], axis=0).astype(BF16) for hh in heads]
         for i in nch]
    qy = [[_dot(bot[i][hh], z[i][hh]) for hh in heads] for i in nch]
    ghs = [_dot(jnp.concatenate(z[i], axis=1), bek[i], TN) for i in nch]
    gh = [[ghs[i][hh * LANES:(hh + 1) * LANES] for hh in heads] for i in nch]
    for i in nch:
        j, p = chains[i]
        qt_ref[p, j * L:(j + 1) * L, :] = rh[i] + jnp.where(first, qy[i][0], swap(qy[i][1]))
        y0_ref[p, j * L:(j + 1) * L, :] = jnp.where(first, swap(qy[i][0]), qy[i][1])
        gh_ref[p, j * LANES:(j + 1) * LANES, :] = jnp.concatenate(
            [only(gh[i][0][:HEAD], 0), only(gh[i][1][:HEAD], 1)], axis=0).astype(BF16)
        ht_ref[p, j * L:(j + 1) * L, :] = jnp.where(first, gh[i][0][HEAD:], gh[i][1][HEAD:])

    def chunk_step(jj, carry):
        cj = jj + d * (n - 1 - 2 * jj)
        r0 = pl.multiple_of(cj * L, L)
        g0 = pl.multiple_of(cj * LANES, LANES)
        e0 = pl.multiple_of(cj * SUBLANES, SUBLANES)
        s = [s_ref[p] for p in range(npair)]
        sbd = [bd(x) for x in s]
        y = [_dot(qt_ref[p, pl.ds(r0, L), :].astype(BF16), sbd[p], NT) for p in range(npair)]
        sg = [_dot(s[p].astype(BF16), gh_ref[p, pl.ds(g0, LANES), :]) for p in range(npair)]
        for p in range(npair):
            y_ref[0, 0, pl.ds(r0, L), p * LANES:(p + 1) * LANES] = y[p] + y0_ref[p, pl.ds(r0, L), :]
            decay = et_ref[pl.ds(e0, SUBLANES), p * LANES:(p + 1) * LANES][0:1, :]
            s_ref[p] = s[p] * decay + sg[p] + ht_ref[p, pl.ds(r0, L), :]
        return carry

    lax.fori_loop(0, n, chunk_step, 0, unroll=True)

    @pl.when(blk == nb - 1)
    def _():
        for p in range(npair):
            for hh in range(HEADS_PER_TILE):
                sf_ref[0, 0, HEADS_PER_TILE * p + hh] = s_ref[p][:, hh * HEAD:(hh + 1) * HEAD]


def _rwkv_scan_call(r, k, v, lw, a, s0, p, tb, hpb):
    B, T, D = r.shape
    L = SCAN_CHUNK
    n = tb // L
    nb = T // tb
    wl = hpb * HEAD
    npair = hpb // HEADS_PER_TILE

    def bidx(d, c):
        return c + d * (nb - 1 - 2 * c)

    row_spec = pl.BlockSpec((1, tb, wl), lambda b, d, h, c: (b, bidx(d, c), h))
    dir_spec = pl.BlockSpec((1, 1, tb, wl), lambda b, d, h, c: (d, b, bidx(d, c), h))
    st_spec = pl.BlockSpec((1, 1, hpb, HEAD, HEAD), lambda b, d, h, c: (b, d, h, 0, 0))
    vec_spec = pl.BlockSpec((1, wl), lambda b, d, h, c: (0, h))
    return pl.pallas_call(
        functools.partial(_scan_kernel, L=L, n=n, nb=nb, hpb=hpb),
        grid=(B, 2, HEADS // hpb, nb),
        in_specs=[row_spec, row_spec, row_spec, dir_spec, dir_spec, st_spec, vec_spec, vec_spec, vec_spec],
        out_specs=[dir_spec, dir_spec, st_spec],
        out_shape=[jax.ShapeDtypeStruct((2, B, T, D), F32), jax.ShapeDtypeStruct((2, B, T, D), F32),
                   jax.ShapeDtypeStruct((B, 2, HEADS, HEAD, HEAD), F32)],
        scratch_shapes=[pltpu.VMEM((npair, HEAD, LANES), F32),
                        pltpu.VMEM((npair, tb, LANES), F32),
                        pltpu.VMEM((npair, tb, LANES), F32),
                        pltpu.VMEM((npair, n * LANES, LANES), BF16),
                        pltpu.VMEM((npair, tb, LANES), F32),
                        pltpu.VMEM((n * SUBLANES, wl), F32)],
        compiler_params=_params("parallel", "parallel", "parallel", "arbitrary"),
        name="rwkv_scan",
    )(r, k, v, lw, a, s0, p["k_k"], p["k_a"], p["r_k"])


def _seg_mean(x, e_ref, et_ref):
    xh, xl = _split2(x)
    s = (_dot(xh, e_ref[...]) + _dot(xl, e_ref[...])) * (1.0 / HEAD)
    sh, slo = _split2(s)
    return _dot(sh, et_ref[...]) + _dot(slo, et_ref[...])


MLP_CHUNKS = 4


def _mlp_apply(x, mod_ref, g_ref, w1_ref, w2_ref):
    h = _norm_mod(x, g_ref[...], mod_ref[0, 3:4, :], mod_ref[0, 4:5, :]).astype(BF16)
    fc = D_FF // MLP_CHUNKS
    acc = jnp.zeros(x.shape, F32)
    for j in range(MLP_CHUNKS):
        hid = jnp.maximum(_dot(h, w1_ref[:, j * fc:(j + 1) * fc]), 0.0)
        acc = acc + _dot((hid * hid).astype(BF16), w2_ref[j * fc:(j + 1) * fc, :])
    return x + mod_ref[0, 5:6, :] * acc


def _mlp_specs():
    return [_const_spec((1, D_MODEL)), _const_spec((D_MODEL, D_FF)), _const_spec((D_FF, D_MODEL))]


def _rwkv_post_kernel(y_ref, bon_ref, gate_ref, x_ref, mod_ref, lnw_ref, lnb_ref, e_ref, et_ref, wo_ref,
                      g2_ref, w1_ref, w2_ref, o_ref):
    y = y_ref[0, 0] + y_ref[1, 0]
    yc = y - _seg_mean(y, e_ref, et_ref)
    var = _seg_mean(yc * yc, e_ref, et_ref)
    yn = yc * lax.rsqrt(var + GN_EPS) * lnw_ref[...] + lnb_ref[...]
    o = (yn + (bon_ref[0, 0] + bon_ref[1, 0])) * gate_ref[0]
    x = x_ref[0] + mod_ref[0, 2:3, :] * _dot(o.astype(BF16), wo_ref[...])
    o_ref[0] = _mlp_apply(x, mod_ref, g2_ref, w1_ref, w2_ref)


def _rwkv_post_call(y, bon, gate, x, mod, p, seg, g2, w1, w2, tm):
    B, T, D = x.shape
    row_spec = pl.BlockSpec((1, tm, D), lambda b, i: (b, i, 0))
    dir_spec = pl.BlockSpec((2, 1, tm, D), lambda b, i: (0, b, i, 0))
    return pl.pallas_call(
        _rwkv_post_kernel,
        grid=(B, T // tm),
        in_specs=[dir_spec, dir_spec, row_spec, row_spec, pl.BlockSpec((1, 6, D), lambda b, i: (b, 0, 0)),
                  _const_spec((1, D)), _const_spec((1, D)), _const_spec((D, LANES)), _const_spec((LANES, D)),
                  _const_spec((D, D))] + _mlp_specs(),
        out_specs=row_spec,
        out_shape=jax.ShapeDtypeStruct((B, T, D), F32),
        compiler_params=_params("parallel", "parallel"),
        name="rwkv_post_mlp",
    )(y, bon, gate, x, mod, p["ln_w"], p["ln_b"], seg[0], seg[1], p["w_o"], g2, w1, w2)


def _na_qkv_kernel(x_ref, mod_ref, g_ref, w_ref, o_ref):
    h = _norm_mod(x_ref[0], g_ref[...], mod_ref[0, 0:1, :], mod_ref[0, 1:2, :]).astype(BF16)
    o_ref[0] = _dot(h, w_ref[...])


def _na_qkv_call(x, mod, g, w, tm):
    B, T, D = x.shape
    return pl.pallas_call(
        _na_qkv_kernel,
        grid=(B, T // tm),
        in_specs=[pl.BlockSpec((1, tm, D), lambda b, i: (b, i, 0)), pl.BlockSpec((1, 6, D), lambda b, i: (b, 0, 0)),
                  _const_spec((1, D)), _const_spec((D, 3 * D))],
        out_specs=pl.BlockSpec((1, tm, 3 * D), lambda b, i: (b, i, 0)),
        out_shape=jax.ShapeDtypeStruct((B, T, 3 * D), F32),
        compiler_params=_params("parallel", "parallel"),
        name="na_qkv",
    )(x, mod, g, w)


def _head_ones():
    lrow = lax.broadcasted_iota(jnp.int32, (LANES, LANES), 0)
    lcol = lax.broadcasted_iota(jnp.int32, (LANES, LANES), 1)
    return jnp.where((lrow // HEAD) == (lcol // HEAD), 1.0, 0.0).astype(BF16)


def _tile_head_norm(x, g, head_ones):
    xh, xl = _split2(x * x)
    ms = (_dot(xh, head_ones) + _dot(xl, head_ones)) * (1.0 / HEAD)
    return x * lax.rsqrt(ms + NORM_EPS) * g


def _stack_heads(x, first_head):
    zero = jnp.zeros_like(x)
    return jnp.concatenate([jnp.where(first_head, x, zero), jnp.where(first_head, zero, x)], axis=0)


def _na_ctx_kernel(qkv_ref, qg_ref, kg_ref, o_ref, kc_ref, vc_ref):
    T = qkv_ref.shape[1]
    head_ones = _head_ones()
    first_head = lax.broadcasted_iota(jnp.int32, (T, LANES), 1) < HEAD
    tiles = range(N_HEAD_TILES)

    def part(j, t):
        return qkv_ref[0, :, (j * N_HEAD_TILES + t) * LANES:(j * N_HEAD_TILES + t + 1) * LANES]

    q = [(_tile_head_norm(part(0, t), qg_ref[...], head_ones) * ATTN_SCALE).astype(BF16) for t in tiles]
    k = [_tile_head_norm(part(1, t), kg_ref[...], head_ones) for t in tiles]
    v = [part(2, t) for t in tiles]
    for t in tiles:
        for hh in range(HEADS_PER_TILE):
            kc_ref[0, HEADS_PER_TILE * t + hh] = k[t][:, hh * HEAD:(hh + 1) * HEAD]
            vc_ref[0, HEADS_PER_TILE * t + hh] = v[t][:, hh * HEAD:(hh + 1) * HEAD]
    s = [_dot(_stack_heads(q[t], first_head), k[t].astype(BF16), NT) for t in tiles]
    e = [jnp.exp(s[t] - jnp.max(s[t], axis=-1, keepdims=True)) for t in tiles]
    o = [_dot(e[t].astype(BF16), v[t].astype(BF16)) / jnp.sum(e[t], axis=-1, keepdims=True) for t in tiles]
    for t in tiles:
        o_ref[0, :, t * LANES:(t + 1) * LANES] = jnp.where(first_head, o[t][:T], o[t][T:])


def _qkv_specs(T):
    return [pl.BlockSpec((1, T, LANES), lambda b, h, part=part: (b, 0, part * N_HEAD_TILES + h)) for part in range(3)]


def _na_ctx_call(qkv, qg, kg):
    B, T, _ = qkv.shape
    cache_spec = pl.BlockSpec((1, HEADS, T, HEAD), lambda b: (b, 0, 0, 0))
    cache_shape = jax.ShapeDtypeStruct((B, HEADS, T, HEAD), F32)
    return pl.pallas_call(
        _na_ctx_kernel,
        grid=(B,),
        in_specs=[pl.BlockSpec((1, T, 3 * D_MODEL), lambda b: (b, 0, 0)), _const_spec((1, LANES)),
                  _const_spec((1, LANES))],
        out_specs=[pl.BlockSpec((1, T, D_MODEL), lambda b: (b, 0, 0)), cache_spec, cache_spec],
        out_shape=[jax.ShapeDtypeStruct((B, T, D_MODEL), F32), cache_shape, cache_shape],
        compiler_params=_params("parallel"),
        name="na_ctx",
    )(qkv, jnp.tile(qg, (1, HEADS_PER_TILE)), jnp.tile(kg, (1, HEADS_PER_TILE)))


def _na_lat_kernel(q_ref, k_ref, v_ref, kc_ref, vc_ref, bias_ref, qg_ref, kg_ref, o_ref, qn_ref, kn_ref, vn_ref,
                   *, rows, rg):
    head_ones = _head_ones()
    qn_ref[...] = (_tile_head_norm(q_ref[0], qg_ref[...], head_ones) * ATTN_SCALE).astype(BF16)
    kn_ref[...] = _tile_head_norm(k_ref[0], kg_ref[...], head_ones).astype(BF16)
    vn_ref[...] = v_ref[0].astype(BF16)
    win = WIN_ROWS * GRID_W
    kc2 = jnp.concatenate([kc_ref[0, hh] for hh in range(HEADS_PER_TILE)], axis=1).astype(BF16)
    vc2 = jnp.concatenate([vc_ref[0, hh] for hh in range(HEADS_PER_TILE)], axis=1).astype(BF16)
    first_head = lax.broadcasted_iota(jnp.int32, (GRID_W, LANES), 1) < HEAD
    nrg = range(rg)

    def row_group(gi, carry):
        q0, k0, var = [], [], []
        for rr in nrg:
            r = gi * rg + rr
            rs = jnp.clip(r - WIN_ROWS // 2, 0, rows - WIN_ROWS)
            var.append(rs - r + WIN_ROWS - 1)
            q0.append(pl.multiple_of(r * GRID_W, GRID_W))
            k0.append(pl.multiple_of(rs * GRID_W, GRID_W))
        qt = [qn_ref[pl.ds(q0[rr], GRID_W), :] for rr in nrg]
        q = [_stack_heads(qt[rr], first_head) for rr in nrg]
        s_w = [_dot(q[rr], kn_ref[pl.ds(k0[rr], win), :], NT) + bias_ref[0, var[rr]] for rr in nrg]
        s_c = [_dot(q[rr], kc2, NT) for rr in nrg]
        m = [jnp.maximum(jnp.max(s_w[rr], axis=-1, keepdims=True), jnp.max(s_c[rr], axis=-1, keepdims=True))
             for rr in nrg]
        e_w = [jnp.exp(s_w[rr] - m[rr]) for rr in nrg]
        e_c = [jnp.exp(s_c[rr] - m[rr]) for rr in nrg]
        den = [jnp.sum(e_w[rr], axis=-1, keepdims=True) + jnp.sum(e_c[rr], axis=-1, keepdims=True) for rr in nrg]
        o = [(_dot(e_w[rr].astype(BF16), vn_ref[pl.ds(k0[rr], win), :]) + _dot(e_c[rr].astype(BF16), vc2)) / den[rr]
             for rr in nrg]
        for rr in nrg:
            o_ref[0, pl.ds(q0[rr], GRID_W), :] = jnp.where(first_head, o[rr][:GRID_W], o[rr][GRID_W:])
        return carry

    lax.fori_loop(0, rows // rg, row_group, 0)


def _na_lat_call(qkv, k_ctx, v_ctx, bias, qg, kg):
    B, T, _ = qkv.shape
    P = k_ctx.shape[2]
    rows = T // GRID_W
    ctx_spec = pl.BlockSpec((1, HEADS_PER_TILE, P, HEAD), lambda b, h: (b, h, 0, 0))
    return pl.pallas_call(
        functools.partial(_na_lat_kernel, rows=rows, rg=8),
        grid=(B, N_HEAD_TILES),
        in_specs=_qkv_specs(T) + [
            ctx_spec, ctx_spec,
            pl.BlockSpec((1, WIN_ROWS, HEADS_PER_TILE * GRID_W, WIN_ROWS * GRID_W), lambda b, h: (h, 0, 0, 0)),
            _const_spec((1, LANES)), _const_spec((1, LANES))],
        out_specs=pl.BlockSpec((1, T, LANES), lambda b, h: (b, 0, h)),
        out_shape=jax.ShapeDtypeStruct((B, T, D_MODEL), F32),
        scratch_shapes=[pltpu.VMEM((T, LANES), BF16)] * 3,
        compiler_params=_params("parallel", "parallel"),
        name="na_lat",
    )(qkv, qkv, qkv, k_ctx, v_ctx, bias, jnp.tile(qg, (1, HEADS_PER_TILE)), jnp.tile(kg, (1, HEADS_PER_TILE)))


def _na_bias_table(rpb):
    n_col = 2 * WIN_COLS - 1
    qc = np.arange(GRID_W)[:, None]
    kc = np.arange(GRID_W)[None, :]
    col = np.clip(kc - qc + WIN_COLS - 1, 0, n_col - 1)
    ws = np.clip(qc - WIN_COLS // 2, 0, GRID_W - WIN_COLS)
    valid = (kc >= ws) & (kc < ws + WIN_COLS)
    onehot = ((col[None] == np.arange(n_col)[:, None, None]) & valid[None]).astype(np.float32)
    mask_bias = np.where(valid, 0.0, NEG_BIG).astype(np.float32)
    rows = jnp.stack([rpb[:, al:al + WIN_ROWS, :] for al in range(WIN_ROWS)], axis=1)
    tab = jnp.einsum("hajc,cqk->haqjk", rows, jnp.asarray(onehot), precision=lax.Precision.HIGHEST)
    tab = tab + jnp.asarray(mask_bias)[None, None, :, None, :]
    tab = tab.reshape(N_HEAD_TILES, HEADS_PER_TILE, WIN_ROWS, GRID_W, WIN_ROWS * GRID_W)
    return tab.transpose(0, 2, 1, 3, 4).reshape(N_HEAD_TILES, WIN_ROWS, HEADS_PER_TILE * GRID_W, WIN_ROWS * GRID_W)


def _out_proj_kernel(a_ref, x_ref, mod_ref, w_ref, g2_ref, w1_ref, w2_ref, o_ref):
    x = x_ref[0] + mod_ref[0, 2:3, :] * _dot(a_ref[0].astype(BF16), w_ref[...])
    o_ref[0] = _mlp_apply(x, mod_ref, g2_ref, w1_ref, w2_ref)


def _out_proj_call(a, x, mod, w, g2, w1, w2, tm):
    B, T, D = x.shape
    row_spec = pl.BlockSpec((1, tm, D), lambda b, i: (b, i, 0))
    return pl.pallas_call(
        _out_proj_kernel,
        grid=(B, T // tm),
        in_specs=[row_spec, row_spec, pl.BlockSpec((1, 6, D), lambda b, i: (b, 0, 0)), _const_spec((D, D))]
        + _mlp_specs(),
        out_specs=row_spec,
        out_shape=jax.ShapeDtypeStruct((B, T, D), F32),
        compiler_params=_params("parallel", "parallel"),
        name="out_proj_mlp",
    )(a, x, mod, w, g2, w1, w2)


def _pad_lora_in(w):
    return jnp.pad(w, ((0, 0), (0, LORA_PAD - w.shape[1])))


def _rwkv_layer_params(i, mu, w_rkv, w_o, w0, w1, w2, a0, a1, a2, g1, g2, k_k, k_a, r_k, ln_w, ln_b):
    rank = w1.shape[-1]
    wl1 = jnp.stack([jnp.concatenate([w1[i, 0], w1[i, 1]], axis=1),
                     jnp.concatenate([a1[i, 0], a1[i, 1]], axis=1),
                     _pad_lora_in(g1[i])]).astype(BF16)
    zeros = jnp.zeros((rank, D_MODEL), F32)
    w2p = jnp.stack([jnp.concatenate([w2[i, 0], zeros]), jnp.concatenate([zeros, w2[i, 1]])]).astype(BF16)
    a2p = jnp.stack([jnp.concatenate([a2[i, 0], zeros]), jnp.concatenate([zeros, a2[i, 1]])]).astype(BF16)
    g2p = jnp.pad(g2[i], ((0, LORA_PAD - g2.shape[1]), (0, 0))).astype(BF16)
    return dict(mu=mu[i], w_rkv=w_rkv[i].astype(BF16), wl1=wl1, w2p=w2p, w0=w0[i], a2p=a2p, a0=a0[i], g2=g2p,
                k_k=k_k[i][None], k_a=k_a[i][None], r_k=r_k[i].reshape(1, D_MODEL),
                ln_w=ln_w[i][None], ln_b=ln_b[i][None], w_o=w_o[i].astype(BF16))


def _segment_matrices():
    e = (np.arange(D_MODEL)[:, None] // HEAD == np.arange(LANES)[None, :]).astype(np.float32)
    return jnp.asarray(e, BF16), jnp.asarray(e.T, BF16)


def kernel(x_prompt, x_sample, state_rwkv, cache_na_k, cache_na_v, c, c_ctx, norm_g, ada_w, ada_b, mlp_w1, mlp_w2, rwkv_mu, rwkv_w_rkv, rwkv_w_o, rwkv_w0, rwkv_w1, rwkv_w2, rwkv_a0, rwkv_a1, rwkv_a2, rwkv_g1, rwkv_g2, rwkv_k_k, rwkv_k_a, rwkv_r_k, rwkv_ln_w, rwkv_ln_b, na_w_qkv, na_w_o, na_q_g, na_k_g, na_rpb):
    n_dec = c.shape[0]
    bp = x_prompt.shape[0]
    cond_rows = 16
    cond = jnp.zeros((cond_rows, D_MODEL), F32).at[:n_dec].set(c).at[n_dec].set(c_ctx)
    mods = _ada_call(cond, ada_w, ada_b)
    mod_lat = mods[:, :n_dec].reshape(DEPTH, n_dec, 6, D_MODEL)
    mod_ctx = jnp.broadcast_to(mods[:, n_dec].reshape(DEPTH, 1, 6, D_MODEL), (DEPTH, bp, 6, D_MODEL))

    rwkv_raw = (rwkv_mu, rwkv_w_rkv, rwkv_w_o, rwkv_w0, rwkv_w1, rwkv_w2, rwkv_a0, rwkv_a1, rwkv_a2,
                rwkv_g1, rwkv_g2, rwkv_k_k, rwkv_k_a, rwkv_r_k, rwkv_ln_w, rwkv_ln_b)
    n_rwkv = rwkv_mu.shape[0]
    n_na = na_w_qkv.shape[0]
    rwkv_p = [_rwkv_layer_params(i, *rwkv_raw) for i in range(n_rwkv)]
    seg = _segment_matrices()
    w1_bf = mlp_w1.astype(BF16)
    w2_bf = mlp_w2.astype(BF16)
    wqkv_bf = na_w_qkv.astype(BF16)
    wo_bf = na_w_o.astype(BF16)
    bias_tabs = [_na_bias_table(na_rpb[i]) for i in range(n_na)]

    def run(x, mod_all, tm, s0_fn, attn_fn):
        states = []
        for l in range(DEPTH):
            i = l // 2
            mod = mod_all[l]
            mlp = (norm_g[l, 1][None], w1_bf[l], w2_bf[l])
            if l % 2 == 0:
                p = rwkv_p[i]
                r, k, v, gate, lw, a = _rwkv_proj_call(x, mod, norm_g[l, 0][None], p, tm)
                y, bon, s_fin = _rwkv_scan_call(r, k, v, lw, a, s0_fn(i), p, 256, 8)
                states.append(s_fin)
                x = _rwkv_post_call(y, bon, gate, x, mod, p, seg, *mlp, 256)
            else:
                qkv = _na_qkv_call(x, mod, norm_g[l, 0][None], wqkv_bf[i], tm)
                o = attn_fn(i, qkv)
                x = _out_proj_call(o, x, mod, wo_bf[i], *mlp, tm)
        return x, states

    new_k, new_v = [], []

    def ctx_attn(i, qkv):
        o, k_c, v_c = _na_ctx_call(qkv, na_q_g[i][None], na_k_g[i][None])
        new_k.append(k_c)
        new_v.append(v_c)
        return o

    def lat_attn(i, qkv):
        return _na_lat_call(qkv, cache_na_k[:, i], cache_na_v[:, i], bias_tabs[i], na_q_g[i][None], na_k_g[i][None])

    zero_state = jnp.zeros((bp, 2, HEADS, HEAD, HEAD), F32)
    y_prompt, new_states = run(x_prompt, mod_ctx, 256, lambda i: zero_state, ctx_attn)
    y_sample, _ = run(x_sample, mod_lat, 512, lambda i: state_rwkv[:, i], lat_attn)
    return (y_prompt, y_sample, jnp.stack(new_states, axis=1), jnp.stack(new_k, axis=1), jnp.stack(new_v, axis=1))
```

```python
import functools

import jax
import jax.numpy as jnp
import numpy as np
from jax import lax
from jax.experimental import pallas as pl
from jax.experimental.pallas import tpu as pltpu

F32 = jnp.float32
BF16 = jnp.bfloat16

D_MODEL = 1024
DEPTH = 4
HEADS = 16
HEAD = 64
LANES = 128
HEADS_PER_TILE = LANES // HEAD
N_HEAD_TILES = HEADS // HEADS_PER_TILE
SUBLANES = 8
D_FF = 4 * D_MODEL
LORA_PAD = 128
GRID_W = 64
WIN_ROWS = 8
WIN_COLS = 16
NORM_EPS = 1e-6
GN_EPS = 64e-5
ATTN_SCALE = HEAD ** -0.5
NEG_BIG = -1e30
DECAY_SCALE = float(np.exp(-0.5))
SCAN_CHUNK = 64
VMEM_LIMIT = 56 * 1024 * 1024

NN = ((1,), (0,))
NT = ((1,), (1,))
TN = ((0,), (0,))


def _dot(a, b, dims=NN):
    return lax.dot_general(a, b, (dims, ((), ())), preferred_element_type=F32)


def _split2(x):
    hi = x.astype(BF16)
    lo = (x - hi.astype(F32)).astype(BF16)
    return hi, lo


def _dot3(a, b, dims=NN):
    ah, al = _split2(a)
    bh, bl = _split2(b)
    return _dot(ah, bh, dims) + (_dot(ah, bl, dims) + _dot(al, bh, dims))


def _params(*sem):
    return pltpu.CompilerParams(dimension_semantics=sem, vmem_limit_bytes=VMEM_LIMIT)


def _norm_mod(x, g, shift, scale):
    ms = jnp.mean(x * x, axis=-1, keepdims=True)
    return (x * lax.rsqrt(ms + NORM_EPS) * g) * (1.0 + scale) + shift


def _const_spec(shape):
    nd = len(shape)
    return pl.BlockSpec(shape, lambda *_: (0,) * nd, pipeline_mode=pl.Buffered(1))


def _ada_kernel(c_ref, w_ref, b_ref, o_ref):
    c = c_ref[...]
    s = c * jax.nn.sigmoid(c)
    o_ref[0] = _dot3(s, w_ref[0]) + b_ref[0]


def _ada_call(cond, ada_w, ada_b):
    rows = cond.shape[0]
    tn = 768
    n = 6 * D_MODEL
    return pl.pallas_call(
        _ada_kernel,
        grid=(DEPTH, n // tn),
        in_specs=[
            pl.BlockSpec((rows, D_MODEL), lambda l, j: (0, 0)),
            pl.BlockSpec((1, D_MODEL, tn), lambda l, j: (l, 0, j)),
            pl.BlockSpec((1, 1, tn), lambda l, j: (l, 0, j)),
        ],
        out_specs=pl.BlockSpec((1, rows, tn), lambda l, j: (l, 0, j)),
        out_shape=jax.ShapeDtypeStruct((DEPTH, rows, n), F32),
        compiler_params=_params("parallel", "parallel"),
        name="ada_mod",
    )(cond, ada_w, ada_b.reshape(DEPTH, 1, n))


def _rwkv_proj_kernel(x_ref, xp_ref, xn_ref, mod_ref, g_ref, mu_ref, wrkv_ref, wl1_ref, w2p_ref, w0_ref,
                      a2p_ref, a0_ref, g2_ref, r_ref, k_ref, v_ref, gate_ref, lw_ref, a_ref, *, tm, nt):
    i = pl.program_id(1)
    g = g_ref[...]
    shift = mod_ref[0, 0:1, :]
    scale = mod_ref[0, 1:2, :]
    h = _norm_mod(x_ref[0], g, shift, scale)
    hp = _norm_mod(xp_ref[0], g, shift, scale)[SUBLANES - 1:SUBLANES, :]
    hn = _norm_mod(xn_ref[0], g, shift, scale)[0:1, :]
    hp = jnp.where(i == 0, 0.0, hp)
    hn = jnp.where(i == nt - 1, 0.0, hn)
    row = lax.broadcasted_iota(jnp.int32, (tm, 1), 0)
    prev = jnp.where(row == 0, hp, pltpu.roll(h, 1, 0))
    nxt = jnp.where(row == tm - 1, hn, pltpu.roll(h, tm - 1, 0))
    delta = 0.5 * (prev + nxt) - h

    def mix(j):
        return (h + delta * mu_ref[j:j + 1, :]).astype(BF16)

    r_ref[0] = _dot(mix(0), wrkv_ref[0])
    k_ref[0] = _dot(mix(1), wrkv_ref[1])
    v_ref[0] = _dot(mix(2), wrkv_ref[2])
    lw = jnp.tanh(_dot(mix(3), wl1_ref[0])).astype(BF16)
    la = _dot(mix(4), wl1_ref[1]).astype(BF16)
    lg = jax.nn.sigmoid(_dot(mix(5), wl1_ref[2])).astype(BF16)
    gate_ref[0] = _dot(lg, g2_ref[...])
    for d in range(2):
        z = w0_ref[d:d + 1, :] + _dot(lw, w2p_ref[d])
        lw_ref[d, 0] = -DECAY_SCALE * jax.nn.sigmoid(z)
        a_ref[d, 0] = jax.nn.sigmoid(a0_ref[d:d + 1, :] + _dot(la, a2p_ref[d]))


def _rwkv_proj_call(x, mod, g, p, tm):
    B, T, D = x.shape
    nt = T // tm
    tb = tm // SUBLANES
    row_spec = pl.BlockSpec((1, tm, D), lambda b, i: (b, i, 0))
    dir_spec = pl.BlockSpec((2, 1, tm, D), lambda b, i: (0, b, i, 0))
    out_bt = jax.ShapeDtypeStruct((B, T, D), F32)
    out_dir = jax.ShapeDtypeStruct((2, B, T, D), F32)
    return pl.pallas_call(
        functools.partial(_rwkv_proj_kernel, tm=tm, nt=nt),
        grid=(B, nt),
        in_specs=[
            row_spec,
            pl.BlockSpec((1, SUBLANES, D), lambda b, i: (b, jnp.maximum(i * tb - 1, 0), 0)),
            pl.BlockSpec((1, SUBLANES, D), lambda b, i: (b, jnp.minimum((i + 1) * tb, T // SUBLANES - 1), 0)),
            pl.BlockSpec((1, 6, D), lambda b, i: (b, 0, 0)),
            _const_spec((1, D)),
            _const_spec((6, D)),
            _const_spec((3, D, D)),
            _const_spec((3, D, LORA_PAD)),
            _const_spec((2, LORA_PAD, D)),
            _const_spec((2, D)),
            _const_spec((2, LORA_PAD, D)),
            _const_spec((2, D)),
            _const_spec((LORA_PAD, D)),
        ],
        out_specs=[row_spec, row_spec, row_spec, row_spec, dir_spec, dir_spec],
        out_shape=[out_bt, out_bt, out_bt, out_bt, out_dir, out_dir],
        compiler_params=_params("parallel", "parallel"),
        name="rwkv_proj",
    )(x, x, x, mod, g, p["mu"], p["w_rkv"], p["wl1"], p["w2p"], p["w0"], p["a2p"], p["a0"], p["g2"])


def _scan_kernel(r_ref, k_ref, v_ref, lw_ref, a_ref, s0_ref, kk_ref, ka_ref, rk_ref,
                 y_ref, bon_ref, sf_ref, s_ref, qt_ref, y0_ref, gh_ref, ht_ref, et_ref, *, L, n, nb, hpb):
    d = pl.program_id(1)
    blk = pl.program_id(3)
    TB = n * L

    npair = hpb // HEADS_PER_TILE

    @pl.when(blk == 0)
    def _():
        for p in range(npair):
            s_ref[p] = jnp.concatenate([s0_ref[0, 0, HEADS_PER_TILE * p + hh] for hh in range(HEADS_PER_TILE)], axis=1)

    sign = 1 - 2 * d
    row = lax.broadcasted_iota(jnp.int32, (TB, TB), 0)
    col = lax.broadcasted_iota(jnp.int32, (TB, TB), 1)
    tri = jnp.where(((row // L) == (col // L)) & ((row - col) * sign >= 0), 1.0, 0.0).astype(BF16)
    lrow = lax.broadcasted_iota(jnp.int32, (LANES, LANES), 0)
    lcol = lax.broadcasted_iota(jnp.int32, (LANES, LANES), 1)
    head_ones = jnp.where((lrow // HEAD) == (lcol // HEAD), 1.0, 0.0).astype(BF16)
    wrow = lax.broadcasted_iota(jnp.int32, (L, LANES), 0)
    wcol = lax.broadcasted_iota(jnp.int32, (L, LANES), 1)
    gorder = (wrow - wcol % L) * sign
    strict = gorder > 0
    incl = gorder >= 0
    first = wcol < HEAD
    eye_w = jnp.where(wrow == wcol % HEAD, 1.0, 0.0)

    def seg_sum(x):
        xb = x.astype(BF16)
        tiles = [slice(t * LANES, (t + 1) * LANES) for t in range(npair)]
        return jnp.concatenate([_dot(xb[:, t], head_ones) for t in tiles], axis=1)

    def cumsum_rows(x):
        xh, xl = _split2(x)
        return _dot(tri, xh) + _dot(tri, xl)

    r2, k2, v2 = r_ref[0], k_ref[0], v_ref[0]
    lw2, a2 = lw_ref[0, 0], a_ref[0, 0]
    kkr = k2 * kk_ref[...]
    kk = kkr * lax.rsqrt(seg_sum(kkr * kkr) + 1e-12)
    kd = k2 * (1.0 + (a2 - 1.0) * ka_ref[...])
    b2 = kk * a2
    bon_ref[0, 0] = seg_sum(r2 * kd * rk_ref[...]) * v2

    cum = cumsum_rows(lw2)
    tot_rows = [jnp.where(d == 0, cum[(j + 1) * L - 1:(j + 1) * L], cum[j * L:j * L + 1]) for j in range(n)]
    tot = jnp.concatenate([jnp.broadcast_to(x, (L, x.shape[1])) for x in tot_rows], axis=0)
    al2 = -kk * jnp.exp(cum - lw2)
    rh2 = r2 * jnp.exp(cum)
    e_neg = jnp.exp(-cum)
    bc2 = b2 * e_neg
    kc2 = kd * e_neg
    e_tail = jnp.exp(tot - cum)
    be2 = b2 * e_tail
    ke2 = kd * e_tail
    for j in range(n):
        et_ref[j * SUBLANES:(j + 1) * SUBLANES, :] = jnp.broadcast_to(jnp.exp(tot_rows[j]), (SUBLANES, tot.shape[1]))

    chains = [(j, p) for j in range(n) for p in range(npair)]
    nch = range(len(chains))

    def part(x, i):
        j, p = chains[i]
        return x[j * L:(j + 1) * L, p * LANES:(p + 1) * LANES]

    def swap(x):
        return pltpu.roll(x, HEAD, 1)

    def only(x, hh):
        return jnp.where(first, x, 0.0) if hh == 0 else jnp.where(first, 0.0, x)

    def bd(x):
        return jnp.concatenate([only(x, 0), only(x, 1)], axis=0).astype(BF16)

    heads = range(HEADS_PER_TILE)
    zeros_w = jnp.zeros((L, LANES), F32)
    al = [part(al2, i) for i in nch]
    rh = [part(rh2, i) for i in nch]
    v = [part(v2, i) for i in nch]
    lhs = [jnp.concatenate([only(al[i], hh) for hh in heads] + [only(rh[i], hh) for hh in heads], axis=0).astype(BF16)
           for i in nch]
    rhs = [jnp.concatenate([part(bc2, i), part(kc2, i)], axis=0).astype(BF16) for i in nch]
    bek = [jnp.concatenate([part(be2, i), part(ke2, i)], axis=0).astype(BF16) for i in nch]
    gram = [_dot(lhs[i], rhs[i], NT) for i in nch]
    top = [[jnp.where(strict, gram[i][hh * L:(hh + 1) * L], 0.0) for hh in heads] for i in nch]
    bot = [[jnp.where(incl, gram[i][(2 + hh) * L:(3 + hh) * L], 0.0).astype(BF16) for hh in heads]
           for i in nch]
    mab = [jnp.where(first, top[i][0], swap(top[i][1])) for i in nch]
    mak = [jnp.where(first, swap(top[i][0]), top[i][1]) for i in nch]
    w = [_dot(mak[i].astype(BF16), bd(v[i])) for i in nch]
    pw = [mab[i].astype(BF16) for i in nch]
    t = [eye_w + pw[i].astype(F32) for i in nch]
    pw = [_dot(pw[i], bd(pw[i])).astype(BF16) for i in nch]
    span = 4
    while span < L:
        both = [_dot(jnp.concatenate([pw[i], t[i].astype(BF16)], axis=0), bd(pw[i])) for i in nch]
        pw = [both[i][:L].astype(BF16) for i in nch]
        t = [t[i] + both[i][L:] for i in nch]
        span *= 2
    t = [t[i] + _dot(t[i].astype(BF16), bd(pw[i])) for i in nch]
    x0 = [jnp.where(first, al[i], swap(w[i])) for i in nch]
    x1 = [jnp.where(first, swap(al[i]), w[i]) for i in nch]
    xbd = [jnp.concatenate([jnp.concatenate([x0[i], zeros_w], axis=1),
                            jnp.concatenate([zeros_w, x1[i]], axis=1)], axis=0).astype(BF16) for i in nch]
    tx = [_dot(t[i].astype(BF16), xbd[i]) for i in nch]
    vz = [[jnp.where(first, 0.0, swap(v[i])), jnp.where(first, 0.0, v[i])] for i in nch]
    z = [[jnp.concatenate([tx[i][:, hh * LANES:(hh + 1) * LANES], vz[i][hh]], axis=0).astype(BF16) for hh in heads]
         for i in nch]
    qy = [[_dot(bot[i][hh], z[i][hh]) for hh in heads] for i in nch]
    ghs = [_dot(jnp.concatenate(z[i], axis=1), bek[i], TN) for i in nch]
    gh = [[ghs[i][hh * LANES:(hh + 1) * LANES] for hh in heads] for i in nch]
    for i in nch:
        j, p = chains[i]
        qt_ref[p, j * L:(j + 1) * L, :] = rh[i] + jnp.where(first, qy[i][0], swap(qy[i][1]))
        y0_ref[p, j * L:(j + 1) * L, :] = jnp.where(first, swap(qy[i][0]), qy[i][1])
        gh_ref[p, j * LANES:(j + 1) * LANES, :] = jnp.concatenate(
            [only(gh[i][0][:HEAD], 0), only(gh[i][1][:HEAD], 1)], axis=0).astype(BF16)
        ht_ref[p, j * L:(j + 1) * L, :] = jnp.where(first, gh[i][0][HEAD:], gh[i][1][HEAD:])

    def chunk_step(jj, carry):
        cj = jj + d * (n - 1 - 2 * jj)
        r0 = pl.multiple_of(cj * L, L)
        g0 = pl.multiple_of(cj * LANES, LANES)
        e0 = pl.multiple_of(cj * SUBLANES, SUBLANES)
        s = [s_ref[p] for p in range(npair)]
        sbd = [bd(x) for x in s]
        y = [_dot(qt_ref[p, pl.ds(r0, L), :].astype(BF16), sbd[p], NT) for p in range(npair)]
        sg = [_dot(s[p].astype(BF16), gh_ref[p, pl.ds(g0, LANES), :]) for p in range(npair)]
        for p in range(npair):
            y_ref[0, 0, pl.ds(r0, L), p * LANES:(p + 1) * LANES] = y[p] + y0_ref[p, pl.ds(r0, L), :]
            decay = et_ref[pl.ds(e0, SUBLANES), p * LANES:(p + 1) * LANES][0:1, :]
            s_ref[p] = s[p] * decay + sg[p] + ht_ref[p, pl.ds(r0, L), :]
        return carry

    lax.fori_loop(0, n, chunk_step, 0, unroll=True)

    @pl.when(blk == nb - 1)
    def _():
        for p in range(npair):
            for hh in range(HEADS_PER_TILE):
                sf_ref[0, 0, HEADS_PER_TILE * p + hh] = s_ref[p][:, hh * HEAD:(hh + 1) * HEAD]


def _rwkv_scan_call(r, k, v, lw, a, s0, p, tb, hpb):
    B, T, D = r.shape
    L = SCAN_CHUNK
    n = tb // L
    nb = T // tb
    wl = hpb * HEAD
    npair = hpb // HEADS_PER_TILE

    def bidx(d, c):
        return c + d * (nb - 1 - 2 * c)

    row_spec = pl.BlockSpec((1, tb, wl), lambda b, d, h, c: (b, bidx(d, c), h))
    dir_spec = pl.BlockSpec((1, 1, tb, wl), lambda b, d, h, c: (d, b, bidx(d, c), h))
    st_spec = pl.BlockSpec((1, 1, hpb, HEAD, HEAD), lambda b, d, h, c: (b, d, h, 0, 0))
    vec_spec = pl.BlockSpec((1, wl), lambda b, d, h, c: (0, h))
    return pl.pallas_call(
        functools.partial(_scan_kernel, L=L, n=n, nb=nb, hpb=hpb),
        grid=(B, 2, HEADS // hpb, nb),
        in_specs=[row_spec, row_spec, row_spec, dir_spec, dir_spec, st_spec, vec_spec, vec_spec, vec_spec],
        out_specs=[dir_spec, dir_spec, st_spec],
        out_shape=[jax.ShapeDtypeStruct((2, B, T, D), F32), jax.ShapeDtypeStruct((2, B, T, D), F32),
                   jax.ShapeDtypeStruct((B, 2, HEADS, HEAD, HEAD), F32)],
        scratch_shapes=[pltpu.VMEM((npair, HEAD, LANES), F32),
                        pltpu.VMEM((npair, tb, LANES), F32),
                        pltpu.VMEM((npair, tb, LANES), F32),
                        pltpu.VMEM((npair, n * LANES, LANES), BF16),
                        pltpu.VMEM((npair, tb, LANES), F32),
                        pltpu.VMEM((n * SUBLANES, wl), F32)],
        compiler_params=_params("parallel", "parallel", "parallel", "arbitrary"),
        name="rwkv_scan",
    )(r, k, v, lw, a, s0, p["k_k"], p["k_a"], p["r_k"])


def _seg_mean(x, e_ref, et_ref):
    xh, xl = _split2(x)
    s = (_dot(xh, e_ref[...]) + _dot(xl, e_ref[...])) * (1.0 / HEAD)
    sh, slo = _split2(s)
    return _dot(sh, et_ref[...]) + _dot(slo, et_ref[...])


def _seg_mean_nonneg(x, e_ref, et_ref):
    s = _dot(x.astype(BF16), e_ref[...]) * (1.0 / HEAD)
    return _dot(s.astype(BF16), et_ref[...])


MLP_CHUNKS = 4


def _mlp_apply(x, mod_ref, g_ref, w1_ref, w2_ref):
    h = _norm_mod(x, g_ref[...], mod_ref[0, 3:4, :], mod_ref[0, 4:5, :]).astype(BF16)
    fc = D_FF // MLP_CHUNKS
    acc = jnp.zeros(x.shape, F32)
    for j in range(MLP_CHUNKS):
        hid = jnp.maximum(_dot(h, w1_ref[:, j * fc:(j + 1) * fc]), 0.0)
        acc = acc + _dot((hid * hid).astype(BF16), w2_ref[j * fc:(j + 1) * fc, :])
    return x + mod_ref[0, 5:6, :] * acc


def _mlp_specs():
    return [_const_spec((1, D_MODEL)), _const_spec((D_MODEL, D_FF)), _const_spec((D_FF, D_MODEL))]


def _rwkv_post_kernel(y_ref, bon_ref, gate_ref, x_ref, mod_ref, lnw_ref, lnb_ref, e_ref, et_ref, wo_ref,
                      g2_ref, w1_ref, w2_ref, o_ref):
    y = y_ref[0, 0] + y_ref[1, 0]
    yc = y - _seg_mean(y, e_ref, et_ref)
    var = _seg_mean_nonneg(yc * yc, e_ref, et_ref)
    yn = yc * lax.rsqrt(var + GN_EPS) * lnw_ref[...] + lnb_ref[...]
    o = (yn + (bon_ref[0, 0] + bon_ref[1, 0])) * gate_ref[0]
    x = x_ref[0] + mod_ref[0, 2:3, :] * _dot(o.astype(BF16), wo_ref[...])
    o_ref[0] = _mlp_apply(x, mod_ref, g2_ref, w1_ref, w2_ref)


def _rwkv_post_call(y, bon, gate, x, mod, p, seg, g2, w1, w2, tm):
    B, T, D = x.shape
    row_spec = pl.BlockSpec((1, tm, D), lambda b, i: (b, i, 0))
    dir_spec = pl.BlockSpec((2, 1, tm, D), lambda b, i: (0, b, i, 0))
    return pl.pallas_call(
        _rwkv_post_kernel,
        grid=(B, T // tm),
        in_specs=[dir_spec, dir_spec, row_spec, row_spec, pl.BlockSpec((1, 6, D), lambda b, i: (b, 0, 0)),
                  _const_spec((1, D)), _const_spec((1, D)), _const_spec((D, LANES)), _const_spec((LANES, D)),
                  _const_spec((D, D))] + _mlp_specs(),
        out_specs=row_spec,
        out_shape=jax.ShapeDtypeStruct((B, T, D), F32),
        compiler_params=_params("parallel", "parallel"),
        name="rwkv_post_mlp",
    )(y, bon, gate, x, mod, p["ln_w"], p["ln_b"], seg[0], seg[1], p["w_o"], g2, w1, w2)


def _na_qkv_kernel(x_ref, mod_ref, g_ref, w_ref, o_ref):
    h = _norm_mod(x_ref[0], g_ref[...], mod_ref[0, 0:1, :], mod_ref[0, 1:2, :]).astype(BF16)
    o_ref[0] = _dot(h, w_ref[...])


def _na_qkv_call(x, mod, g, w, tm):
    B, T, D = x.shape
    return pl.pallas_call(
        _na_qkv_kernel,
        grid=(B, T // tm),
        in_specs=[pl.BlockSpec((1, tm, D), lambda b, i: (b, i, 0)), pl.BlockSpec((1, 6, D), lambda b, i: (b, 0, 0)),
                  _const_spec((1, D)), _const_spec((D, 3 * D))],
        out_specs=pl.BlockSpec((1, tm, 3 * D), lambda b, i: (b, i, 0)),
        out_shape=jax.ShapeDtypeStruct((B, T, 3 * D), F32),
        compiler_params=_params("parallel", "parallel"),
        name="na_qkv",
    )(x, mod, g, w)


def _head_ones():
    lrow = lax.broadcasted_iota(jnp.int32, (LANES, LANES), 0)
    lcol = lax.broadcasted_iota(jnp.int32, (LANES, LANES), 1)
    return jnp.where((lrow // HEAD) == (lcol // HEAD), 1.0, 0.0).astype(BF16)


def _tile_head_norm(x, g, head_ones):
    ms = _dot((x * x).astype(BF16), head_ones) * (1.0 / HEAD)
    return x * lax.rsqrt(ms + NORM_EPS) * g


def _stack_heads(x, first_head):
    zero = jnp.zeros_like(x)
    return jnp.concatenate([jnp.where(first_head, x, zero), jnp.where(first_head, zero, x)], axis=0)


def _na_ctx_kernel(qkv_ref, qg_ref, kg_ref, k_in_ref, v_in_ref, o_ref, kc_ref, vc_ref):
    del k_in_ref, v_in_ref
    T = qkv_ref.shape[1]
    head_ones = _head_ones()
    first_head = lax.broadcasted_iota(jnp.int32, (T, LANES), 1) < HEAD
    tiles = range(N_HEAD_TILES)

    def part(j, t):
        return qkv_ref[0, :, (j * N_HEAD_TILES + t) * LANES:(j * N_HEAD_TILES + t + 1) * LANES]

    q = [(_tile_head_norm(part(0, t), qg_ref[...], head_ones) * ATTN_SCALE).astype(BF16) for t in tiles]
    k = [_tile_head_norm(part(1, t), kg_ref[...], head_ones) for t in tiles]
    v = [part(2, t) for t in tiles]
    for t in tiles:
        for hh in range(HEADS_PER_TILE):
            kc_ref[0, 0, HEADS_PER_TILE * t + hh] = k[t][:, hh * HEAD:(hh + 1) * HEAD]
            vc_ref[0, 0, HEADS_PER_TILE * t + hh] = v[t][:, hh * HEAD:(hh + 1) * HEAD]
    s = [_dot(_stack_heads(q[t], first_head), k[t].astype(BF16), NT) for t in tiles]
    e = [jnp.exp(s[t] - jnp.max(s[t], axis=-1, keepdims=True)) for t in tiles]
    o = [_dot(e[t].astype(BF16), v[t].astype(BF16)) / jnp.sum(e[t], axis=-1, keepdims=True) for t in tiles]
    for t in tiles:
        o_ref[0, :, t * LANES:(t + 1) * LANES] = jnp.where(first_head, o[t][:T], o[t][T:])


def _qkv_specs(T):
    return [pl.BlockSpec((1, T, LANES), lambda b, h, part=part: (b, 0, part * N_HEAD_TILES + h)) for part in range(3)]


def _na_ctx_call(qkv, qg, kg, k_cache, v_cache, layer):
    B, T, _ = qkv.shape
    cache_spec = pl.BlockSpec((1, 1, HEADS, T, HEAD), lambda b: (b, layer, 0, 0, 0))
    cache_shape = jax.ShapeDtypeStruct(k_cache.shape, F32)
    any_spec = pl.BlockSpec(memory_space=pl.ANY)
    return pl.pallas_call(
        _na_ctx_kernel,
        grid=(B,),
        in_specs=[pl.BlockSpec((1, T, 3 * D_MODEL), lambda b: (b, 0, 0)), _const_spec((1, LANES)),
                  _const_spec((1, LANES)), any_spec, any_spec],
        out_specs=[pl.BlockSpec((1, T, D_MODEL), lambda b: (b, 0, 0)), cache_spec, cache_spec],
        out_shape=[jax.ShapeDtypeStruct((B, T, D_MODEL), F32), cache_shape, cache_shape],
        input_output_aliases={3: 1, 4: 2},
        compiler_params=_params("parallel"),
        name="na_ctx",
    )(qkv, jnp.tile(qg, (1, HEADS_PER_TILE)), jnp.tile(kg, (1, HEADS_PER_TILE)), k_cache, v_cache)


def _na_lat_kernel(q_ref, k_ref, v_ref, kc_ref, vc_ref, bias_ref, qg_ref, kg_ref, o_ref, qn_ref, kn_ref, vn_ref,
                   *, rows, rg):
    head_ones = _head_ones()
    qn_ref[...] = (_tile_head_norm(q_ref[0], qg_ref[...], head_ones) * ATTN_SCALE).astype(BF16)
    kn_ref[...] = _tile_head_norm(k_ref[0], kg_ref[...], head_ones).astype(BF16)
    vn_ref[...] = v_ref[0].astype(BF16)
    win = WIN_ROWS * GRID_W
    kc2 = jnp.concatenate([kc_ref[0, 0, hh] for hh in range(HEADS_PER_TILE)], axis=1).astype(BF16)
    vc2 = jnp.concatenate([vc_ref[0, 0, hh] for hh in range(HEADS_PER_TILE)], axis=1).astype(BF16)
    first_head = lax.broadcasted_iota(jnp.int32, (GRID_W, LANES), 1) < HEAD
    nrg = range(rg)

    def row_group(gi, carry):
        q0, k0, var = [], [], []
        for rr in nrg:
            r = gi * rg + rr
            rs = jnp.clip(r - WIN_ROWS // 2, 0, rows - WIN_ROWS)
            var.append(rs - r + WIN_ROWS - 1)
            q0.append(pl.multiple_of(r * GRID_W, GRID_W))
            k0.append(pl.multiple_of(rs * GRID_W, GRID_W))
        qt = [qn_ref[pl.ds(q0[rr], GRID_W), :] for rr in nrg]
        q = [_stack_heads(qt[rr], first_head) for rr in nrg]
        s_w = [_dot(q[rr], kn_ref[pl.ds(k0[rr], win), :], NT) + bias_ref[0, var[rr]] for rr in nrg]
        s_c = [_dot(q[rr], kc2, NT) for rr in nrg]
        m = [jnp.maximum(jnp.max(s_w[rr], axis=-1, keepdims=True), jnp.max(s_c[rr], axis=-1, keepdims=True))
             for rr in nrg]
        e_w = [jnp.exp(s_w[rr] - m[rr]) for rr in nrg]
        e_c = [jnp.exp(s_c[rr] - m[rr]) for rr in nrg]
        den = [jnp.sum(e_w[rr], axis=-1, keepdims=True) + jnp.sum(e_c[rr], axis=-1, keepdims=True) for rr in nrg]
        o = [(_dot(e_w[rr].astype(BF16), vn_ref[pl.ds(k0[rr], win), :]) + _dot(e_c[rr].astype(BF16), vc2)) / den[rr]
             for rr in nrg]
        for rr in nrg:
            o_ref[0, pl.ds(q0[rr], GRID_W), :] = jnp.where(first_head, o[rr][:GRID_W], o[rr][GRID_W:])
        return carry

    lax.fori_loop(0, rows // rg, row_group, 0)


def _na_lat_call(qkv, k_ctx, v_ctx, layer, bias, qg, kg):
    B, T, _ = qkv.shape
    P = k_ctx.shape[3]
    rows = T // GRID_W
    ctx_spec = pl.BlockSpec((1, 1, HEADS_PER_TILE, P, HEAD), lambda b, h: (b, layer, h, 0, 0))
    return pl.pallas_call(
        functools.partial(_na_lat_kernel, rows=rows, rg=8),
        grid=(B, N_HEAD_TILES),
        in_specs=_qkv_specs(T) + [
            ctx_spec, ctx_spec,
            pl.BlockSpec((1, WIN_ROWS, HEADS_PER_TILE * GRID_W, WIN_ROWS * GRID_W), lambda b, h: (h, 0, 0, 0)),
            _const_spec((1, LANES)), _const_spec((1, LANES))],
        out_specs=pl.BlockSpec((1, T, LANES), lambda b, h: (b, 0, h)),
        out_shape=jax.ShapeDtypeStruct((B, T, D_MODEL), F32),
        scratch_shapes=[pltpu.VMEM((T, LANES), BF16)] * 3,
        compiler_params=_params("parallel", "parallel"),
        name="na_lat",
    )(qkv, qkv, qkv, k_ctx, v_ctx, bias, jnp.tile(qg, (1, HEADS_PER_TILE)), jnp.tile(kg, (1, HEADS_PER_TILE)))


def _na_bias_table(rpb):
    n_col = 2 * WIN_COLS - 1
    qc = np.arange(GRID_W)[:, None]
    kc = np.arange(GRID_W)[None, :]
    col = np.clip(kc - qc + WIN_COLS - 1, 0, n_col - 1)
    ws = np.clip(qc - WIN_COLS // 2, 0, GRID_W - WIN_COLS)
    valid = (kc >= ws) & (kc < ws + WIN_COLS)
    onehot = ((col[None] == np.arange(n_col)[:, None, None]) & valid[None]).astype(np.float32)
    mask_bias = np.where(valid, 0.0, NEG_BIG).astype(np.float32)
    rows = jnp.stack([rpb[:, al:al + WIN_ROWS, :] for al in range(WIN_ROWS)], axis=1)
    tab = jnp.einsum("hajc,cqk->haqjk", rows, jnp.asarray(onehot), precision=lax.Precision.HIGHEST)
    tab = tab + jnp.asarray(mask_bias)[None, None, :, None, :]
    tab = tab.reshape(N_HEAD_TILES, HEADS_PER_TILE, WIN_ROWS, GRID_W, WIN_ROWS * GRID_W)
    return tab.transpose(0, 2, 1, 3, 4).reshape(N_HEAD_TILES, WIN_ROWS, HEADS_PER_TILE * GRID_W, WIN_ROWS * GRID_W)


def _out_proj_kernel(a_ref, x_ref, mod_ref, w_ref, g2_ref, w1_ref, w2_ref, o_ref):
    x = x_ref[0] + mod_ref[0, 2:3, :] * _dot(a_ref[0].astype(BF16), w_ref[...])
    o_ref[0] = _mlp_apply(x, mod_ref, g2_ref, w1_ref, w2_ref)


def _out_proj_call(a, x, mod, w, g2, w1, w2, tm):
    B, T, D = x.shape
    row_spec = pl.BlockSpec((1, tm, D), lambda b, i: (b, i, 0))
    return pl.pallas_call(
        _out_proj_kernel,
        grid=(B, T // tm),
        in_specs=[row_spec, row_spec, pl.BlockSpec((1, 6, D), lambda b, i: (b, 0, 0)), _const_spec((D, D))]
        + _mlp_specs(),
        out_specs=row_spec,
        out_shape=jax.ShapeDtypeStruct((B, T, D), F32),
        compiler_params=_params("parallel", "parallel"),
        name="out_proj_mlp",
    )(a, x, mod, w, g2, w1, w2)


def _pad_lora_in(w):
    return jnp.pad(w, ((0, 0), (0, LORA_PAD - w.shape[1])))


def _rwkv_layer_params(i, mu, w_rkv, w_o, w0, w1, w2, a0, a1, a2, g1, g2, k_k, k_a, r_k, ln_w, ln_b):
    rank = w1.shape[-1]
    wl1 = jnp.stack([jnp.concatenate([w1[i, 0], w1[i, 1]], axis=1),
                     jnp.concatenate([a1[i, 0], a1[i, 1]], axis=1),
                     _pad_lora_in(g1[i])]).astype(BF16)
    zeros = jnp.zeros((rank, D_MODEL), F32)
    w2p = jnp.stack([jnp.concatenate([w2[i, 0], zeros]), jnp.concatenate([zeros, w2[i, 1]])]).astype(BF16)
    a2p = jnp.stack([jnp.concatenate([a2[i, 0], zeros]), jnp.concatenate([zeros, a2[i, 1]])]).astype(BF16)
    g2p = jnp.pad(g2[i], ((0, LORA_PAD - g2.shape[1]), (0, 0))).astype(BF16)
    return dict(mu=mu[i], w_rkv=w_rkv[i].astype(BF16), wl1=wl1, w2p=w2p, w0=w0[i], a2p=a2p, a0=a0[i], g2=g2p,
                k_k=k_k[i][None], k_a=k_a[i][None], r_k=r_k[i].reshape(1, D_MODEL),
                ln_w=ln_w[i][None], ln_b=ln_b[i][None], w_o=w_o[i].astype(BF16))


def _segment_matrices():
    e = (np.arange(D_MODEL)[:, None] // HEAD == np.arange(LANES)[None, :]).astype(np.float32)
    return jnp.asarray(e, BF16), jnp.asarray(e.T, BF16)


def kernel(x_prompt, x_sample, state_rwkv, cache_na_k, cache_na_v, c, c_ctx, norm_g, ada_w, ada_b, mlp_w1, mlp_w2, rwkv_mu, rwkv_w_rkv, rwkv_w_o, rwkv_w0, rwkv_w1, rwkv_w2, rwkv_a0, rwkv_a1, rwkv_a2, rwkv_g1, rwkv_g2, rwkv_k_k, rwkv_k_a, rwkv_r_k, rwkv_ln_w, rwkv_ln_b, na_w_qkv, na_w_o, na_q_g, na_k_g, na_rpb):
    n_dec = c.shape[0]
    bp = x_prompt.shape[0]
    cond_rows = 16
    cond = jnp.zeros((cond_rows, D_MODEL), F32).at[:n_dec].set(c).at[n_dec].set(c_ctx)
    mods = _ada_call(cond, ada_w, ada_b)
    mod_lat = mods[:, :n_dec].reshape(DEPTH, n_dec, 6, D_MODEL)
    mod_ctx = jnp.broadcast_to(mods[:, n_dec].reshape(DEPTH, 1, 6, D_MODEL), (DEPTH, bp, 6, D_MODEL))

    rwkv_raw = (rwkv_mu, rwkv_w_rkv, rwkv_w_o, rwkv_w0, rwkv_w1, rwkv_w2, rwkv_a0, rwkv_a1, rwkv_a2,
                rwkv_g1, rwkv_g2, rwkv_k_k, rwkv_k_a, rwkv_r_k, rwkv_ln_w, rwkv_ln_b)
    n_rwkv = rwkv_mu.shape[0]
    n_na = na_w_qkv.shape[0]
    rwkv_p = [_rwkv_layer_params(i, *rwkv_raw) for i in range(n_rwkv)]
    seg = _segment_matrices()
    w1_bf = mlp_w1.astype(BF16)
    w2_bf = mlp_w2.astype(BF16)
    wqkv_bf = na_w_qkv.astype(BF16)
    wo_bf = na_w_o.astype(BF16)
    bias_tabs = [_na_bias_table(na_rpb[i]) for i in range(n_na)]

    def run(x, mod_all, tm, s0_fn, attn_fn):
        states = []
        for l in range(DEPTH):
            i = l // 2
            mod = mod_all[l]
            mlp = (norm_g[l, 1][None], w1_bf[l], w2_bf[l])
            if l % 2 == 0:
                p = rwkv_p[i]
                r, k, v, gate, lw, a = _rwkv_proj_call(x, mod, norm_g[l, 0][None], p, tm)
                y, bon, s_fin = _rwkv_scan_call(r, k, v, lw, a, s0_fn(i), p, 256, 16)
                states.append(s_fin)
                x = _rwkv_post_call(y, bon, gate, x, mod, p, seg, *mlp, 256)
            else:
                qkv = _na_qkv_call(x, mod, norm_g[l, 0][None], wqkv_bf[i], tm)
                o = attn_fn(i, qkv)
                x = _out_proj_call(o, x, mod, wo_bf[i], *mlp, tm)
        return x, states

    seq = x_prompt.shape[1]
    new_kv = [jnp.zeros((bp, n_na, HEADS, seq, HEAD), F32), jnp.zeros((bp, n_na, HEADS, seq, HEAD), F32)]

    def ctx_attn(i, qkv):
        o, new_kv[0], new_kv[1] = _na_ctx_call(qkv, na_q_g[i][None], na_k_g[i][None], new_kv[0], new_kv[1], i)
        return o

    def lat_attn(i, qkv):
        return _na_lat_call(qkv, cache_na_k, cache_na_v, i, bias_tabs[i], na_q_g[i][None], na_k_g[i][None])

    zero_state = jnp.zeros((bp, 2, HEADS, HEAD, HEAD), F32)
    y_prompt, new_states = run(x_prompt, mod_ctx, 256, lambda i: zero_state, ctx_attn)
    y_sample, _ = run(x_sample, mod_lat, 512, lambda i: state_rwkv[:, i], lat_attn)
    return (y_prompt, y_sample, jnp.stack(new_states, axis=1), new_kv[0], new_kv[1])
```

```python
import functools

import jax
import jax.numpy as jnp
import numpy as np
from jax import lax
from jax.experimental import pallas as pl
from jax.experimental.pallas import tpu as pltpu

F32 = jnp.float32
BF16 = jnp.bfloat16

D_MODEL = 1024
DEPTH = 4
HEADS = 16
HEAD = 64
LANES = 128
HEADS_PER_TILE = LANES // HEAD
N_HEAD_TILES = HEADS // HEADS_PER_TILE
SUBLANES = 8
D_FF = 4 * D_MODEL
LORA_PAD = 128
GRID_W = 64
WIN_ROWS = 8
WIN_COLS = 16
NORM_EPS = 1e-6
GN_EPS = 64e-5
ATTN_SCALE = HEAD ** -0.5
NEG_BIG = -1e30
DECAY_SCALE = float(np.exp(-0.5))
SCAN_CHUNK = 64
VMEM_LIMIT = 56 * 1024 * 1024

NN = ((1,), (0,))
NT = ((1,), (1,))
TN = ((0,), (0,))


def _dot(a, b, dims=NN):
    return lax.dot_general(a, b, (dims, ((), ())), preferred_element_type=F32)


def _split2(x):
    hi = x.astype(BF16)
    lo = (x - hi.astype(F32)).astype(BF16)
    return hi, lo


def _dot3(a, b, dims=NN):
    ah, al = _split2(a)
    bh, bl = _split2(b)
    return _dot(ah, bh, dims) + (_dot(ah, bl, dims) + _dot(al, bh, dims))


def _params(*sem):
    return pltpu.CompilerParams(dimension_semantics=sem, vmem_limit_bytes=VMEM_LIMIT)


def _norm_mod(x, g, shift, scale):
    ms = jnp.mean(x * x, axis=-1, keepdims=True)
    return (x * lax.rsqrt(ms + NORM_EPS) * g) * (1.0 + scale) + shift


def _const_spec(shape):
    nd = len(shape)
    return pl.BlockSpec(shape, lambda *_: (0,) * nd, pipeline_mode=pl.Buffered(1))


def _ada_kernel(c_ref, w_ref, b_ref, o_ref):
    c = c_ref[...]
    s = c * jax.nn.sigmoid(c)
    o_ref[0] = _dot3(s, w_ref[0]) + b_ref[0]


def _ada_call(cond, ada_w, ada_b):
    rows = cond.shape[0]
    tn = 768
    n = 6 * D_MODEL
    return pl.pallas_call(
        _ada_kernel,
        grid=(DEPTH, n // tn),
        in_specs=[
            pl.BlockSpec((rows, D_MODEL), lambda l, j: (0, 0)),
            pl.BlockSpec((1, D_MODEL, tn), lambda l, j: (l, 0, j)),
            pl.BlockSpec((1, 1, tn), lambda l, j: (l, 0, j)),
        ],
        out_specs=pl.BlockSpec((1, rows, tn), lambda l, j: (l, 0, j)),
        out_shape=jax.ShapeDtypeStruct((DEPTH, rows, n), F32),
        compiler_params=_params("parallel", "parallel"),
        name="ada_mod",
    )(cond, ada_w, ada_b.reshape(DEPTH, 1, n))


def _rwkv_proj_kernel(x_ref, xp_ref, xn_ref, mod_ref, g_ref, mu_ref, wrkv_ref, wl1_ref, w2p_ref, w0_ref,
                      a2p_ref, a0_ref, g2_ref, r_ref, k_ref, v_ref, gate_ref, lw_ref, a_ref, *, tm, nt):
    i = pl.program_id(1)
    g = g_ref[...]
    shift = mod_ref[0, 0:1, :]
    scale = mod_ref[0, 1:2, :]
    h = _norm_mod(x_ref[0], g, shift, scale)
    hp = _norm_mod(xp_ref[0], g, shift, scale)[SUBLANES - 1:SUBLANES, :]
    hn = _norm_mod(xn_ref[0], g, shift, scale)[0:1, :]
    hp = jnp.where(i == 0, 0.0, hp)
    hn = jnp.where(i == nt - 1, 0.0, hn)
    row = lax.broadcasted_iota(jnp.int32, (tm, 1), 0)
    prev = jnp.where(row == 0, hp, pltpu.roll(h, 1, 0))
    nxt = jnp.where(row == tm - 1, hn, pltpu.roll(h, tm - 1, 0))
    delta = 0.5 * (prev + nxt) - h

    def mix(j):
        return (h + delta * mu_ref[j:j + 1, :]).astype(BF16)

    r_ref[0] = _dot(mix(0), wrkv_ref[0])
    k_ref[0] = _dot(mix(1), wrkv_ref[1])
    v_ref[0] = _dot(mix(2), wrkv_ref[2])
    lw = jnp.tanh(_dot(mix(3), wl1_ref[0])).astype(BF16)
    la = _dot(mix(4), wl1_ref[1]).astype(BF16)
    lg = jax.nn.sigmoid(_dot(mix(5), wl1_ref[2])).astype(BF16)
    gate_ref[0] = _dot(lg, g2_ref[...])
    for d in range(2):
        z = w0_ref[d:d + 1, :] + _dot(lw, w2p_ref[d])
        lw_ref[d, 0] = -DECAY_SCALE * jax.nn.sigmoid(z)
        a_ref[d, 0] = jax.nn.sigmoid(a0_ref[d:d + 1, :] + _dot(la, a2p_ref[d]))


def _rwkv_proj_call(x, mod, g, p, tm):
    B, T, D = x.shape
    nt = T // tm
    tb = tm // SUBLANES
    row_spec = pl.BlockSpec((1, tm, D), lambda b, i: (b, i, 0))
    dir_spec = pl.BlockSpec((2, 1, tm, D), lambda b, i: (0, b, i, 0))
    out_bt = jax.ShapeDtypeStruct((B, T, D), F32)
    out_dir = jax.ShapeDtypeStruct((2, B, T, D), F32)
    return pl.pallas_call(
        functools.partial(_rwkv_proj_kernel, tm=tm, nt=nt),
        grid=(B, nt),
        in_specs=[
            row_spec,
            pl.BlockSpec((1, SUBLANES, D), lambda b, i: (b, jnp.maximum(i * tb - 1, 0), 0)),
            pl.BlockSpec((1, SUBLANES, D), lambda b, i: (b, jnp.minimum((i + 1) * tb, T // SUBLANES - 1), 0)),
            pl.BlockSpec((1, 6, D), lambda b, i: (b, 0, 0)),
            _const_spec((1, D)),
            _const_spec((6, D)),
            _const_spec((3, D, D)),
            _const_spec((3, D, LORA_PAD)),
            _const_spec((2, LORA_PAD, D)),
            _const_spec((2, D)),
            _const_spec((2, LORA_PAD, D)),
            _const_spec((2, D)),
            _const_spec((LORA_PAD, D)),
        ],
        out_specs=[row_spec, row_spec, row_spec, row_spec, dir_spec, dir_spec],
        out_shape=[out_bt, out_bt, out_bt, out_bt, out_dir, out_dir],
        compiler_params=_params("parallel", "parallel"),
        name="rwkv_proj",
    )(x, x, x, mod, g, p["mu"], p["w_rkv"], p["wl1"], p["w2p"], p["w0"], p["a2p"], p["a0"], p["g2"])


def _scan_kernel(r_ref, k_ref, v_ref, lw_ref, a_ref, s0_ref, kk_ref, ka_ref, rk_ref,
                 y_ref, bon_ref, sf_ref, s_ref, qt_ref, y0_ref, gh_ref, ht_ref, et_ref, *, L, n, nb, hpb):
    d = pl.program_id(1)
    blk = pl.program_id(3)
    TB = n * L

    npair = hpb // HEADS_PER_TILE

    @pl.when(blk == 0)
    def _():
        for p in range(npair):
            s_ref[p] = jnp.concatenate([s0_ref[0, 0, HEADS_PER_TILE * p + hh] for hh in range(HEADS_PER_TILE)], axis=1)

    sign = 1 - 2 * d
    row = lax.broadcasted_iota(jnp.int32, (TB, TB), 0)
    col = lax.broadcasted_iota(jnp.int32, (TB, TB), 1)
    tri = jnp.where(((row // L) == (col // L)) & ((row - col) * sign >= 0), 1.0, 0.0).astype(BF16)
    lrow = lax.broadcasted_iota(jnp.int32, (LANES, LANES), 0)
    lcol = lax.broadcasted_iota(jnp.int32, (LANES, LANES), 1)
    head_ones = jnp.where((lrow // HEAD) == (lcol // HEAD), 1.0, 0.0).astype(BF16)
    wrow = lax.broadcasted_iota(jnp.int32, (L, LANES), 0)
    wcol = lax.broadcasted_iota(jnp.int32, (L, LANES), 1)
    gorder = (wrow - wcol % L) * sign
    strict = gorder > 0
    incl = gorder >= 0
    first = wcol < HEAD
    eye_w = jnp.where(wrow == wcol % HEAD, 1.0, 0.0)

    def seg_sum(x):
        xb = x.astype(BF16)
        tiles = [slice(t * LANES, (t + 1) * LANES) for t in range(npair)]
        return jnp.concatenate([_dot(xb[:, t], head_ones) for t in tiles], axis=1)

    def cumsum_rows(x):
        xh, xl = _split2(x)
        return _dot(tri, xh) + _dot(tri, xl)

    r2, k2, v2 = r_ref[0], k_ref[0], v_ref[0]
    lw2, a2 = lw_ref[0, 0], a_ref[0, 0]
    kkr = k2 * kk_ref[...]
    kk = kkr * lax.rsqrt(seg_sum(kkr * kkr) + 1e-12)
    kd = k2 * (1.0 + (a2 - 1.0) * ka_ref[...])
    b2 = kk * a2
    bon_ref[0, 0] = seg_sum(r2 * kd * rk_ref[...]) * v2

    cum = cumsum_rows(lw2)
    tot_rows = [jnp.where(d == 0, cum[(j + 1) * L - 1:(j + 1) * L], cum[j * L:j * L + 1]) for j in range(n)]
    tot = jnp.concatenate([jnp.broadcast_to(x, (L, x.shape[1])) for x in tot_rows], axis=0)
    al2 = -kk * jnp.exp(cum - lw2)
    rh2 = r2 * jnp.exp(cum)
    e_neg = jnp.exp(-cum)
    bc2 = b2 * e_neg
    kc2 = kd * e_neg
    e_tail = jnp.exp(tot - cum)
    be2 = b2 * e_tail
    ke2 = kd * e_tail
    for j in range(n):
        et_ref[j * SUBLANES:(j + 1) * SUBLANES, :] = jnp.broadcast_to(jnp.exp(tot_rows[j]), (SUBLANES, tot.shape[1]))

    chains = [(j, p) for j in range(n) for p in range(npair)]
    nch = range(len(chains))

    def part(x, i):
        j, p = chains[i]
        return x[j * L:(j + 1) * L, p * LANES:(p + 1) * LANES]

    def swap(x):
        return pltpu.roll(x, HEAD, 1)

    def only(x, hh):
        return jnp.where(first, x, 0.0) if hh == 0 else jnp.where(first, 0.0, x)

    def bd(x):
        return jnp.concatenate([only(x, 0), only(x, 1)], axis=0).astype(BF16)

    heads = range(HEADS_PER_TILE)
    zeros_w = jnp.zeros((L, LANES), F32)
    al = [part(al2, i) for i in nch]
    rh = [part(rh2, i) for i in nch]
    v = [part(v2, i) for i in nch]
    lhs = [jnp.concatenate([only(al[i], hh) for hh in heads] + [only(rh[i], hh) for hh in heads], axis=0).astype(BF16)
           for i in nch]
    rhs = [jnp.concatenate([part(bc2, i), part(kc2, i)], axis=0).astype(BF16) for i in nch]
    bek = [jnp.concatenate([part(be2, i), part(ke2, i)], axis=0).astype(BF16) for i in nch]
    gram = [_dot(lhs[i], rhs[i], NT) for i in nch]
    top = [[jnp.where(strict, gram[i][hh * L:(hh + 1) * L], 0.0) for hh in heads] for i in nch]
    bot = [[jnp.where(incl, gram[i][(2 + hh) * L:(3 + hh) * L], 0.0).astype(BF16) for hh in heads]
           for i in nch]
    mab = [jnp.where(first, top[i][0], swap(top[i][1])) for i in nch]
    mak = [jnp.where(first, swap(top[i][0]), top[i][1]) for i in nch]
    w = [_dot(mak[i].astype(BF16), bd(v[i])) for i in nch]
    pw = [mab[i].astype(BF16) for i in nch]
    t = [eye_w + pw[i].astype(F32) for i in nch]
    pw = [_dot(pw[i], bd(pw[i])).astype(BF16) for i in nch]
    span = 4
    while span < L:
        both = [_dot(jnp.concatenate([pw[i], t[i].astype(BF16)], axis=0), bd(pw[i])) for i in nch]
        pw = [both[i][:L].astype(BF16) for i in nch]
        t = [t[i] + both[i][L:] for i in nch]
        span *= 2
    t = [t[i] + _dot(t[i].astype(BF16), bd(pw[i])) for i in nch]
    x0 = [jnp.where(first, al[i], swap(w[i])) for i in nch]
    x1 = [jnp.where(first, swap(al[i]), w[i]) for i in nch]
    xbd = [jnp.concatenate([jnp.concatenate([x0[i], zeros_w], axis=1),
                            jnp.concatenate([zeros_w, x1[i]], axis=1)], axis=0).astype(BF16) for i in nch]
    tx = [_dot(t[i].astype(BF16), xbd[i]) for i in nch]
    vz = [[jnp.where(first, 0.0, swap(v[i])), jnp.where(first, 0.0, v[i])] for i in nch]
    z = [[jnp.concatenate([tx[i][:, hh * LANES:(hh + 1) * LANES], vz[i][hh]], axis=0).astype(BF16) for hh in heads]
         for i in nch]
    qy = [[_dot(bot[i][hh], z[i][hh]) for hh in heads] for i in nch]
    ghs = [_dot(jnp.concatenate(z[i], axis=1), bek[i], TN) for i in nch]
    gh = [[ghs[i][hh * LANES:(hh + 1) * LANES] for hh in heads] for i in nch]
    for i in nch:
        j, p = chains[i]
        qt_ref[p, j * L:(j + 1) * L, :] = rh[i] + jnp.where(first, qy[i][0], swap(qy[i][1]))
        y0_ref[p, j * L:(j + 1) * L, :] = jnp.where(first, swap(qy[i][0]), qy[i][1])
        gh_ref[p, j * LANES:(j + 1) * LANES, :] = jnp.concatenate(
            [only(gh[i][0][:HEAD], 0), only(gh[i][1][:HEAD], 1)], axis=0).astype(BF16)
        ht_ref[p, j * L:(j + 1) * L, :] = jnp.where(first, gh[i][0][HEAD:], gh[i][1][HEAD:])

    def chunk_step(jj, carry):
        cj = jj + d * (n - 1 - 2 * jj)
        r0 = pl.multiple_of(cj * L, L)
        g0 = pl.multiple_of(cj * LANES, LANES)
        e0 = pl.multiple_of(cj * SUBLANES, SUBLANES)
        s = [s_ref[p] for p in range(npair)]
        sbd = [bd(x) for x in s]
        y = [_dot(qt_ref[p, pl.ds(r0, L), :].astype(BF16), sbd[p], NT) for p in range(npair)]
        sg = [_dot(s[p].astype(BF16), gh_ref[p, pl.ds(g0, LANES), :]) for p in range(npair)]
        for p in range(npair):
            y_ref[0, 0, pl.ds(r0, L), p * LANES:(p + 1) * LANES] = y[p] + y0_ref[p, pl.ds(r0, L), :]
            decay = et_ref[pl.ds(e0, SUBLANES), p * LANES:(p + 1) * LANES][0:1, :]
            s_ref[p] = s[p] * decay + sg[p] + ht_ref[p, pl.ds(r0, L), :]
        return carry

    lax.fori_loop(0, n, chunk_step, 0, unroll=True)

    @pl.when(blk == nb - 1)
    def _():
        for p in range(npair):
            for hh in range(HEADS_PER_TILE):
                sf_ref[0, 0, HEADS_PER_TILE * p + hh] = s_ref[p][:, hh * HEAD:(hh + 1) * HEAD]


def _rwkv_scan_call(r, k, v, lw, a, s0, p, tb, hpb):
    B, T, D = r.shape
    L = SCAN_CHUNK
    n = tb // L
    nb = T // tb
    wl = hpb * HEAD
    npair = hpb // HEADS_PER_TILE

    def bidx(d, c):
        return c + d * (nb - 1 - 2 * c)

    row_spec = pl.BlockSpec((1, tb, wl), lambda b, d, h, c: (b, bidx(d, c), h))
    dir_spec = pl.BlockSpec((1, 1, tb, wl), lambda b, d, h, c: (d, b, bidx(d, c), h))
    st_spec = pl.BlockSpec((1, 1, hpb, HEAD, HEAD), lambda b, d, h, c: (b, d, h, 0, 0))
    vec_spec = pl.BlockSpec((1, wl), lambda b, d, h, c: (0, h))
    return pl.pallas_call(
        functools.partial(_scan_kernel, L=L, n=n, nb=nb, hpb=hpb),
        grid=(B, 2, HEADS // hpb, nb),
        in_specs=[row_spec, row_spec, row_spec, dir_spec, dir_spec, st_spec, vec_spec, vec_spec, vec_spec],
        out_specs=[dir_spec, dir_spec, st_spec],
        out_shape=[jax.ShapeDtypeStruct((2, B, T, D), F32), jax.ShapeDtypeStruct((2, B, T, D), F32),
                   jax.ShapeDtypeStruct((B, 2, HEADS, HEAD, HEAD), F32)],
        scratch_shapes=[pltpu.VMEM((npair, HEAD, LANES), F32),
                        pltpu.VMEM((npair, tb, LANES), F32),
                        pltpu.VMEM((npair, tb, LANES), F32),
                        pltpu.VMEM((npair, n * LANES, LANES), BF16),
                        pltpu.VMEM((npair, tb, LANES), F32),
                        pltpu.VMEM((n * SUBLANES, wl), F32)],
        compiler_params=_params("parallel", "parallel", "parallel", "arbitrary"),
        name="rwkv_scan",
    )(r, k, v, lw, a, s0, p["k_k"], p["k_a"], p["r_k"])


def _seg_mean(x, e_ref, et_ref):
    xh, xl = _split2(x)
    s = (_dot(xh, e_ref[...]) + _dot(xl, e_ref[...])) * (1.0 / HEAD)
    sh, slo = _split2(s)
    return _dot(sh, et_ref[...]) + _dot(slo, et_ref[...])


def _seg_mean_nonneg(x, e_ref, et_ref):
    s = _dot(x.astype(BF16), e_ref[...]) * (1.0 / HEAD)
    return _dot(s.astype(BF16), et_ref[...])


MLP_CHUNKS = 4


def _mlp_apply(x, mod_ref, g_ref, w1_ref, w2_ref):
    h = _norm_mod(x, g_ref[...], mod_ref[0, 3:4, :], mod_ref[0, 4:5, :]).astype(BF16)
    fc = D_FF // MLP_CHUNKS
    acc = jnp.zeros(x.shape, F32)
    for j in range(MLP_CHUNKS):
        hid = jnp.maximum(_dot(h, w1_ref[:, j * fc:(j + 1) * fc]), 0.0)
        acc = acc + _dot((hid * hid).astype(BF16), w2_ref[j * fc:(j + 1) * fc, :])
    return x + mod_ref[0, 5:6, :] * acc


def _mlp_specs():
    return [_const_spec((1, D_MODEL)), _const_spec((D_MODEL, D_FF)), _const_spec((D_FF, D_MODEL))]


def _rwkv_post_kernel(y_ref, bon_ref, gate_ref, x_ref, mod_ref, lnw_ref, lnb_ref, e_ref, et_ref, wo_ref,
                      g2_ref, w1_ref, w2_ref, o_ref):
    y = y_ref[0, 0] + y_ref[1, 0]
    yc = y - _seg_mean(y, e_ref, et_ref)
    var = _seg_mean_nonneg(yc * yc, e_ref, et_ref)
    yn = yc * lax.rsqrt(var + GN_EPS) * lnw_ref[...] + lnb_ref[...]
    o = (yn + (bon_ref[0, 0] + bon_ref[1, 0])) * gate_ref[0]
    x = x_ref[0] + mod_ref[0, 2:3, :] * _dot(o.astype(BF16), wo_ref[...])
    o_ref[0] = _mlp_apply(x, mod_ref, g2_ref, w1_ref, w2_ref)


def _rwkv_post_call(y, bon, gate, x, mod, p, seg, g2, w1, w2, tm):
    B, T, D = x.shape
    row_spec = pl.BlockSpec((1, tm, D), lambda b, i: (b, i, 0))
    dir_spec = pl.BlockSpec((2, 1, tm, D), lambda b, i: (0, b, i, 0))
    return pl.pallas_call(
        _rwkv_post_kernel,
        grid=(B, T // tm),
        in_specs=[dir_spec, dir_spec, row_spec, row_spec, pl.BlockSpec((1, 6, D), lambda b, i: (b, 0, 0)),
                  _const_spec((1, D)), _const_spec((1, D)), _const_spec((D, LANES)), _const_spec((LANES, D)),
                  _const_spec((D, D))] + _mlp_specs(),
        out_specs=row_spec,
        out_shape=jax.ShapeDtypeStruct((B, T, D), F32),
        compiler_params=_params("parallel", "parallel"),
        name="rwkv_post_mlp",
    )(y, bon, gate, x, mod, p["ln_w"], p["ln_b"], seg[0], seg[1], p["w_o"], g2, w1, w2)


def _na_qkv_kernel(x_ref, mod_ref, g_ref, w_ref, o_ref):
    h = _norm_mod(x_ref[0], g_ref[...], mod_ref[0, 0:1, :], mod_ref[0, 1:2, :]).astype(BF16)
    o_ref[0] = _dot(h, w_ref[...])


def _na_qkv_call(x, mod, g, w, tm):
    B, T, D = x.shape
    return pl.pallas_call(
        _na_qkv_kernel,
        grid=(B, T // tm),
        in_specs=[pl.BlockSpec((1, tm, D), lambda b, i: (b, i, 0)), pl.BlockSpec((1, 6, D), lambda b, i: (b, 0, 0)),
                  _const_spec((1, D)), _const_spec((D, 3 * D))],
        out_specs=pl.BlockSpec((1, tm, 3 * D), lambda b, i: (b, i, 0)),
        out_shape=jax.ShapeDtypeStruct((B, T, 3 * D), F32),
        compiler_params=_params("parallel", "parallel"),
        name="na_qkv",
    )(x, mod, g, w)


def _head_ones():
    lrow = lax.broadcasted_iota(jnp.int32, (LANES, LANES), 0)
    lcol = lax.broadcasted_iota(jnp.int32, (LANES, LANES), 1)
    return jnp.where((lrow // HEAD) == (lcol // HEAD), 1.0, 0.0).astype(BF16)


def _tile_head_norm(x, g, head_ones):
    ms = _dot((x * x).astype(BF16), head_ones) * (1.0 / HEAD)
    return x * lax.rsqrt(ms + NORM_EPS) * g


def _stack_heads(x, first_head):
    zero = jnp.zeros_like(x)
    return jnp.concatenate([jnp.where(first_head, x, zero), jnp.where(first_head, zero, x)], axis=0)


def _na_ctx_kernel(qkv_ref, qg_ref, kg_ref, *refs, layer, creates):
    o_ref, kc_ref, vc_ref = refs[-3:]
    slot = layer if creates else 0
    if creates:
        for other in range(kc_ref.shape[1]):
            if other != layer:
                kc_ref[0, other] = jnp.zeros(kc_ref.shape[2:], F32)
                vc_ref[0, other] = jnp.zeros(vc_ref.shape[2:], F32)
    T = qkv_ref.shape[1]
    head_ones = _head_ones()
    first_head = lax.broadcasted_iota(jnp.int32, (T, LANES), 1) < HEAD
    tiles = range(N_HEAD_TILES)

    def part(j, t):
        return qkv_ref[0, :, (j * N_HEAD_TILES + t) * LANES:(j * N_HEAD_TILES + t + 1) * LANES]

    q = [(_tile_head_norm(part(0, t), qg_ref[...], head_ones) * ATTN_SCALE).astype(BF16) for t in tiles]
    k = [_tile_head_norm(part(1, t), kg_ref[...], head_ones) for t in tiles]
    v = [part(2, t) for t in tiles]
    for t in tiles:
        for hh in range(HEADS_PER_TILE):
            kc_ref[0, slot, HEADS_PER_TILE * t + hh] = k[t][:, hh * HEAD:(hh + 1) * HEAD]
            vc_ref[0, slot, HEADS_PER_TILE * t + hh] = v[t][:, hh * HEAD:(hh + 1) * HEAD]
    s = [_dot(_stack_heads(q[t], first_head), k[t].astype(BF16), NT) for t in tiles]
    e = [jnp.exp(s[t] - jnp.max(s[t], axis=-1, keepdims=True)) for t in tiles]
    o = [_dot(e[t].astype(BF16), v[t].astype(BF16)) / jnp.sum(e[t], axis=-1, keepdims=True) for t in tiles]
    for t in tiles:
        o_ref[0, :, t * LANES:(t + 1) * LANES] = jnp.where(first_head, o[t][:T], o[t][T:])


def _qkv_specs(T):
    return [pl.BlockSpec((1, T, LANES), lambda b, h, part=part: (b, 0, part * N_HEAD_TILES + h)) for part in range(3)]


def _na_ctx_call(qkv, qg, kg, caches, layer, n_layers):
    B, T, _ = qkv.shape
    creates = caches is None
    if creates:
        cache_spec = pl.BlockSpec((1, n_layers, HEADS, T, HEAD), lambda b: (b, 0, 0, 0, 0))
        extra_specs, extra_args, aliases = [], (), {}
    else:
        cache_spec = pl.BlockSpec((1, 1, HEADS, T, HEAD), lambda b: (b, layer, 0, 0, 0))
        extra_specs, extra_args, aliases = [pl.BlockSpec(memory_space=pl.ANY)] * 2, tuple(caches), {3: 1, 4: 2}
    cache_shape = jax.ShapeDtypeStruct((B, n_layers, HEADS, T, HEAD), F32)
    return pl.pallas_call(
        functools.partial(_na_ctx_kernel, layer=layer, creates=creates),
        grid=(B,),
        in_specs=[pl.BlockSpec((1, T, 3 * D_MODEL), lambda b: (b, 0, 0)), _const_spec((1, LANES)),
                  _const_spec((1, LANES))] + extra_specs,
        out_specs=[pl.BlockSpec((1, T, D_MODEL), lambda b: (b, 0, 0)), cache_spec, cache_spec],
        out_shape=[jax.ShapeDtypeStruct((B, T, D_MODEL), F32), cache_shape, cache_shape],
        input_output_aliases=aliases,
        compiler_params=_params("parallel"),
        name="na_ctx",
    )(qkv, jnp.tile(qg, (1, HEADS_PER_TILE)), jnp.tile(kg, (1, HEADS_PER_TILE)), *extra_args)


def _na_lat_kernel(q_ref, k_ref, v_ref, kc_ref, vc_ref, bias_ref, qg_ref, kg_ref, o_ref, qn_ref, kn_ref, vn_ref,
                   *, rows, rg):
    head_ones = _head_ones()
    qn_ref[...] = (_tile_head_norm(q_ref[0], qg_ref[...], head_ones) * ATTN_SCALE).astype(BF16)
    kn_ref[...] = _tile_head_norm(k_ref[0], kg_ref[...], head_ones).astype(BF16)
    vn_ref[...] = v_ref[0].astype(BF16)
    win = WIN_ROWS * GRID_W
    kc2 = jnp.concatenate([kc_ref[0, 0, hh] for hh in range(HEADS_PER_TILE)], axis=1).astype(BF16)
    vc2 = jnp.concatenate([vc_ref[0, 0, hh] for hh in range(HEADS_PER_TILE)], axis=1).astype(BF16)
    first_head = lax.broadcasted_iota(jnp.int32, (GRID_W, LANES), 1) < HEAD
    nrg = range(rg)

    def row_group(gi, carry):
        q0, k0, var = [], [], []
        for rr in nrg:
            r = gi * rg + rr
            rs = jnp.clip(r - WIN_ROWS // 2, 0, rows - WIN_ROWS)
            var.append(rs - r + WIN_ROWS - 1)
            q0.append(pl.multiple_of(r * GRID_W, GRID_W))
            k0.append(pl.multiple_of(rs * GRID_W, GRID_W))
        qt = [qn_ref[pl.ds(q0[rr], GRID_W), :] for rr in nrg]
        q = [_stack_heads(qt[rr], first_head) for rr in nrg]
        s_w = [_dot(q[rr], kn_ref[pl.ds(k0[rr], win), :], NT) + bias_ref[0, var[rr]] for rr in nrg]
        s_c = [_dot(q[rr], kc2, NT) for rr in nrg]
        m = [jnp.maximum(jnp.max(s_w[rr], axis=-1, keepdims=True), jnp.max(s_c[rr], axis=-1, keepdims=True))
             for rr in nrg]
        e_w = [jnp.exp(s_w[rr] - m[rr]) for rr in nrg]
        e_c = [jnp.exp(s_c[rr] - m[rr]) for rr in nrg]
        den = [jnp.sum(e_w[rr], axis=-1, keepdims=True) + jnp.sum(e_c[rr], axis=-1, keepdims=True) for rr in nrg]
        o = [(_dot(e_w[rr].astype(BF16), vn_ref[pl.ds(k0[rr], win), :]) + _dot(e_c[rr].astype(BF16), vc2)) / den[rr]
             for rr in nrg]
        for rr in nrg:
            o_ref[0, pl.ds(q0[rr], GRID_W), :] = jnp.where(first_head, o[rr][:GRID_W], o[rr][GRID_W:])
        return carry

    lax.fori_loop(0, rows // rg, row_group, 0)


def _na_lat_call(qkv, k_ctx, v_ctx, layer, bias, qg, kg):
    B, T, _ = qkv.shape
    P = k_ctx.shape[3]
    rows = T // GRID_W
    ctx_spec = pl.BlockSpec((1, 1, HEADS_PER_TILE, P, HEAD), lambda b, h: (b, layer, h, 0, 0))
    return pl.pallas_call(
        functools.partial(_na_lat_kernel, rows=rows, rg=8),
        grid=(B, N_HEAD_TILES),
        in_specs=_qkv_specs(T) + [
            ctx_spec, ctx_spec,
            pl.BlockSpec((1, WIN_ROWS, HEADS_PER_TILE * GRID_W, WIN_ROWS * GRID_W), lambda b, h: (h, 0, 0, 0)),
            _const_spec((1, LANES)), _const_spec((1, LANES))],
        out_specs=pl.BlockSpec((1, T, LANES), lambda b, h: (b, 0, h)),
        out_shape=jax.ShapeDtypeStruct((B, T, D_MODEL), F32),
        scratch_shapes=[pltpu.VMEM((T, LANES), BF16)] * 3,
        compiler_params=_params("parallel", "parallel"),
        name="na_lat",
    )(qkv, qkv, qkv, k_ctx, v_ctx, bias, jnp.tile(qg, (1, HEADS_PER_TILE)), jnp.tile(kg, (1, HEADS_PER_TILE)))


def _na_bias_table(rpb):
    n_col = 2 * WIN_COLS - 1
    qc = np.arange(GRID_W)[:, None]
    kc = np.arange(GRID_W)[None, :]
    col = np.clip(kc - qc + WIN_COLS - 1, 0, n_col - 1)
    ws = np.clip(qc - WIN_COLS // 2, 0, GRID_W - WIN_COLS)
    valid = (kc >= ws) & (kc < ws + WIN_COLS)
    onehot = np.concatenate([(col[None] == np.arange(n_col)[:, None, None]) & valid[None], ~valid[None]], axis=0)
    rows = jnp.stack([rpb[:, al:al + WIN_ROWS, :] for al in range(WIN_ROWS)], axis=1)
    rows = jnp.concatenate([rows, jnp.full(rows.shape[:-1] + (1,), NEG_BIG, F32)], axis=-1)
    rows = rows.reshape(N_HEAD_TILES, HEADS_PER_TILE, WIN_ROWS, WIN_ROWS, n_col + 1)
    tab = jnp.einsum("thajc,cqk->tahqjk", rows, jnp.asarray(onehot.astype(np.float32)),
                     precision=lax.Precision.HIGHEST)
    return tab.reshape(N_HEAD_TILES, WIN_ROWS, HEADS_PER_TILE * GRID_W, WIN_ROWS * GRID_W)


def _out_proj_kernel(a_ref, x_ref, mod_ref, w_ref, g2_ref, w1_ref, w2_ref, o_ref):
    x = x_ref[0] + mod_ref[0, 2:3, :] * _dot(a_ref[0].astype(BF16), w_ref[...])
    o_ref[0] = _mlp_apply(x, mod_ref, g2_ref, w1_ref, w2_ref)


def _out_proj_call(a, x, mod, w, g2, w1, w2, tm):
    B, T, D = x.shape
    row_spec = pl.BlockSpec((1, tm, D), lambda b, i: (b, i, 0))
    return pl.pallas_call(
        _out_proj_kernel,
        grid=(B, T // tm),
        in_specs=[row_spec, row_spec, pl.BlockSpec((1, 6, D), lambda b, i: (b, 0, 0)), _const_spec((D, D))]
        + _mlp_specs(),
        out_specs=row_spec,
        out_shape=jax.ShapeDtypeStruct((B, T, D), F32),
        compiler_params=_params("parallel", "parallel"),
        name="out_proj_mlp",
    )(a, x, mod, w, g2, w1, w2)


def _pad_lora_in(w):
    return jnp.pad(w, ((0, 0), (0, LORA_PAD - w.shape[1])))


def _rwkv_layer_params(i, mu, w_rkv, w_o, w0, w1, w2, a0, a1, a2, g1, g2, k_k, k_a, r_k, ln_w, ln_b):
    rank = w1.shape[-1]
    wl1 = jnp.stack([jnp.concatenate([w1[i, 0], w1[i, 1]], axis=1),
                     jnp.concatenate([a1[i, 0], a1[i, 1]], axis=1),
                     _pad_lora_in(g1[i])]).astype(BF16)
    zeros = jnp.zeros((rank, D_MODEL), F32)
    w2p = jnp.stack([jnp.concatenate([w2[i, 0], zeros]), jnp.concatenate([zeros, w2[i, 1]])]).astype(BF16)
    a2p = jnp.stack([jnp.concatenate([a2[i, 0], zeros]), jnp.concatenate([zeros, a2[i, 1]])]).astype(BF16)
    g2p = jnp.pad(g2[i], ((0, LORA_PAD - g2.shape[1]), (0, 0))).astype(BF16)
    return dict(mu=mu[i], w_rkv=w_rkv[i].astype(BF16), wl1=wl1, w2p=w2p, w0=w0[i], a2p=a2p, a0=a0[i], g2=g2p,
                k_k=k_k[i][None], k_a=k_a[i][None], r_k=r_k[i].reshape(1, D_MODEL),
                ln_w=ln_w[i][None], ln_b=ln_b[i][None], w_o=w_o[i].astype(BF16))


def _segment_matrices():
    e = (np.arange(D_MODEL)[:, None] // HEAD == np.arange(LANES)[None, :]).astype(np.float32)
    return jnp.asarray(e, BF16), jnp.asarray(e.T, BF16)


def kernel(x_prompt, x_sample, state_rwkv, cache_na_k, cache_na_v, c, c_ctx, norm_g, ada_w, ada_b, mlp_w1, mlp_w2, rwkv_mu, rwkv_w_rkv, rwkv_w_o, rwkv_w0, rwkv_w1, rwkv_w2, rwkv_a0, rwkv_a1, rwkv_a2, rwkv_g1, rwkv_g2, rwkv_k_k, rwkv_k_a, rwkv_r_k, rwkv_ln_w, rwkv_ln_b, na_w_qkv, na_w_o, na_q_g, na_k_g, na_rpb):
    n_dec = c.shape[0]
    bp = x_prompt.shape[0]
    cond_rows = 16
    cond = jnp.zeros((cond_rows, D_MODEL), F32).at[:n_dec].set(c).at[n_dec].set(c_ctx)
    mods = _ada_call(cond, ada_w, ada_b)
    mod_lat = mods[:, :n_dec].reshape(DEPTH, n_dec, 6, D_MODEL)
    mod_ctx = jnp.broadcast_to(mods[:, n_dec].reshape(DEPTH, 1, 6, D_MODEL), (DEPTH, bp, 6, D_MODEL))

    rwkv_raw = (rwkv_mu, rwkv_w_rkv, rwkv_w_o, rwkv_w0, rwkv_w1, rwkv_w2, rwkv_a0, rwkv_a1, rwkv_a2,
                rwkv_g1, rwkv_g2, rwkv_k_k, rwkv_k_a, rwkv_r_k, rwkv_ln_w, rwkv_ln_b)
    n_rwkv = rwkv_mu.shape[0]
    n_na = na_w_qkv.shape[0]
    rwkv_p = [_rwkv_layer_params(i, *rwkv_raw) for i in range(n_rwkv)]
    seg = _segment_matrices()
    w1_bf = [mlp_w1[l].astype(BF16) for l in range(DEPTH)]
    w2_bf = [mlp_w2[l].astype(BF16) for l in range(DEPTH)]
    wqkv_bf = [na_w_qkv[i].astype(BF16) for i in range(n_na)]
    wo_bf = [na_w_o[i].astype(BF16) for i in range(n_na)]
    bias_tabs = [_na_bias_table(na_rpb[i]) for i in range(n_na)]

    def run(x, mod_all, tm, s0_fn, attn_fn):
        states = []
        for l in range(DEPTH):
            i = l // 2
            mod = mod_all[l]
            mlp = (norm_g[l, 1][None], w1_bf[l], w2_bf[l])
            if l % 2 == 0:
                p = rwkv_p[i]
                r, k, v, gate, lw, a = _rwkv_proj_call(x, mod, norm_g[l, 0][None], p, tm)
                y, bon, s_fin = _rwkv_scan_call(r, k, v, lw, a, s0_fn(i), p, 256, 16)
                states.append(s_fin)
                x = _rwkv_post_call(y, bon, gate, x, mod, p, seg, *mlp, 256)
            else:
                qkv = _na_qkv_call(x, mod, norm_g[l, 0][None], wqkv_bf[i], tm)
                o = attn_fn(i, qkv)
                x = _out_proj_call(o, x, mod, wo_bf[i], *mlp, tm)
        return x, states

    new_kv = []

    def ctx_attn(i, qkv):
        o, k_c, v_c = _na_ctx_call(qkv, na_q_g[i][None], na_k_g[i][None], new_kv[-1] if new_kv else None, i, n_na)
        new_kv.append((k_c, v_c))
        return o

    def lat_attn(i, qkv):
        return _na_lat_call(qkv, cache_na_k, cache_na_v, i, bias_tabs[i], na_q_g[i][None], na_k_g[i][None])

    zero_state = jnp.zeros((bp, 2, HEADS, HEAD, HEAD), F32)
    y_prompt, new_states = run(x_prompt, mod_ctx, 256, lambda i: zero_state, ctx_attn)
    y_sample, _ = run(x_sample, mod_lat, 512, lambda i: state_rwkv[:, i], lat_attn)
    return (y_prompt, y_sample, jnp.stack(new_states, axis=1), new_kv[-1][0], new_kv[-1][1])
```

```python
import functools

import jax
import jax.numpy as jnp
import numpy as np
from jax import lax
from jax.experimental import pallas as pl
from jax.experimental.pallas import tpu as pltpu

F32 = jnp.float32
BF16 = jnp.bfloat16

D_MODEL = 1024
DEPTH = 4
HEADS = 16
HEAD = 64
LANES = 128
HEADS_PER_TILE = LANES // HEAD
N_HEAD_TILES = HEADS // HEADS_PER_TILE
SUBLANES = 8
D_FF = 4 * D_MODEL
LORA_PAD = 128
GRID_W = 64
WIN_ROWS = 8
WIN_COLS = 16
NORM_EPS = 1e-6
GN_EPS = 64e-5
ATTN_SCALE = HEAD ** -0.5
NEG_BIG = -1e30
DECAY_SCALE = float(np.exp(-0.5))
SCAN_CHUNK = 64
VMEM_LIMIT = 56 * 1024 * 1024

NN = ((1,), (0,))
NT = ((1,), (1,))
TN = ((0,), (0,))


def _dot(a, b, dims=NN):
    return lax.dot_general(a, b, (dims, ((), ())), preferred_element_type=F32)


def _split2(x):
    hi = x.astype(BF16)
    lo = (x - hi.astype(F32)).astype(BF16)
    return hi, lo


def _dot3(a, b, dims=NN):
    ah, al = _split2(a)
    bh, bl = _split2(b)
    return _dot(ah, bh, dims) + (_dot(ah, bl, dims) + _dot(al, bh, dims))


def _params(*sem):
    return pltpu.CompilerParams(dimension_semantics=sem, vmem_limit_bytes=VMEM_LIMIT)


def _norm_mod(x, g, shift, scale):
    ms = jnp.mean(x * x, axis=-1, keepdims=True)
    return (x * lax.rsqrt(ms + NORM_EPS) * g) * (1.0 + scale) + shift


def _const_spec(shape):
    nd = len(shape)
    return pl.BlockSpec(shape, lambda *_: (0,) * nd, pipeline_mode=pl.Buffered(1))


def _ada_kernel(c_ref, w_ref, b_ref, o_ref):
    c = c_ref[...]
    s = c * jax.nn.sigmoid(c)
    o_ref[0] = _dot3(s, w_ref[0]) + b_ref[0]


def _ada_call(cond, ada_w, ada_b):
    rows = cond.shape[0]
    tn = 768
    n = 6 * D_MODEL
    return pl.pallas_call(
        _ada_kernel,
        grid=(DEPTH, n // tn),
        in_specs=[
            pl.BlockSpec((rows, D_MODEL), lambda l, j: (0, 0)),
            pl.BlockSpec((1, D_MODEL, tn), lambda l, j: (l, 0, j)),
            pl.BlockSpec((1, 1, tn), lambda l, j: (l, 0, j)),
        ],
        out_specs=pl.BlockSpec((1, rows, tn), lambda l, j: (l, 0, j)),
        out_shape=jax.ShapeDtypeStruct((DEPTH, rows, n), F32),
        compiler_params=_params("parallel", "parallel"),
        name="ada_mod",
    )(cond, ada_w, ada_b.reshape(DEPTH, 1, n))


def _rwkv_proj_kernel(x_ref, xp_ref, xn_ref, mod_ref, g_ref, mu_ref, wrkv_ref, wl1_ref, w2p_ref, w0_ref,
                      a2p_ref, a0_ref, g2_ref, r_ref, k_ref, v_ref, gate_ref, lw_ref, a_ref, *, tm, nt):
    i = pl.program_id(1)
    g = g_ref[...]
    shift = mod_ref[0, 0:1, :]
    scale = mod_ref[0, 1:2, :]
    h = _norm_mod(x_ref[0], g, shift, scale)
    hp = _norm_mod(xp_ref[0], g, shift, scale)[SUBLANES - 1:SUBLANES, :]
    hn = _norm_mod(xn_ref[0], g, shift, scale)[0:1, :]
    hp = jnp.where(i == 0, 0.0, hp)
    hn = jnp.where(i == nt - 1, 0.0, hn)
    row = lax.broadcasted_iota(jnp.int32, (tm, 1), 0)
    prev = jnp.where(row == 0, hp, pltpu.roll(h, 1, 0))
    nxt = jnp.where(row == tm - 1, hn, pltpu.roll(h, tm - 1, 0))
    delta = 0.5 * (prev + nxt) - h

    def mix(j):
        return (h + delta * mu_ref[j:j + 1, :]).astype(BF16)

    r_ref[0] = _dot(mix(0), wrkv_ref[0])
    k_ref[0] = _dot(mix(1), wrkv_ref[1])
    v_ref[0] = _dot(mix(2), wrkv_ref[2])
    lw = jnp.tanh(_dot(mix(3), wl1_ref[0])).astype(BF16)
    la = _dot(mix(4), wl1_ref[1]).astype(BF16)
    lg = jax.nn.sigmoid(_dot(mix(5), wl1_ref[2])).astype(BF16)
    gate_ref[0] = _dot(lg, g2_ref[...])
    for d in range(2):
        z = w0_ref[d:d + 1, :] + _dot(lw, w2p_ref[d])
        lw_ref[d, 0] = -DECAY_SCALE * jax.nn.sigmoid(z)
        a_ref[d, 0] = jax.nn.sigmoid(a0_ref[d:d + 1, :] + _dot(la, a2p_ref[d]))


def _rwkv_proj_call(x, mod, g, p, tm):
    B, T, D = x.shape
    nt = T // tm
    tb = tm // SUBLANES
    row_spec = pl.BlockSpec((1, tm, D), lambda b, i: (b, i, 0))
    dir_spec = pl.BlockSpec((2, 1, tm, D), lambda b, i: (0, b, i, 0))
    out_bt = jax.ShapeDtypeStruct((B, T, D), F32)
    out_dir = jax.ShapeDtypeStruct((2, B, T, D), F32)
    return pl.pallas_call(
        functools.partial(_rwkv_proj_kernel, tm=tm, nt=nt),
        grid=(B, nt),
        in_specs=[
            row_spec,
            pl.BlockSpec((1, SUBLANES, D), lambda b, i: (b, jnp.maximum(i * tb - 1, 0), 0)),
            pl.BlockSpec((1, SUBLANES, D), lambda b, i: (b, jnp.minimum((i + 1) * tb, T // SUBLANES - 1), 0)),
            pl.BlockSpec((1, 6, D), lambda b, i: (b, 0, 0)),
            _const_spec((1, D)),
            _const_spec((6, D)),
            _const_spec((3, D, D)),
            _const_spec((3, D, LORA_PAD)),
            _const_spec((2, LORA_PAD, D)),
            _const_spec((2, D)),
            _const_spec((2, LORA_PAD, D)),
            _const_spec((2, D)),
            _const_spec((LORA_PAD, D)),
        ],
        out_specs=[row_spec, row_spec, row_spec, row_spec, dir_spec, dir_spec],
        out_shape=[out_bt, out_bt, out_bt, out_bt, out_dir, out_dir],
        compiler_params=_params("parallel", "parallel"),
        name="rwkv_proj",
    )(x, x, x, mod, g, p["mu"], p["w_rkv"], p["wl1"], p["w2p"], p["w0"], p["a2p"], p["a0"], p["g2"])


def _scan_kernel(r_ref, k_ref, v_ref, lw_ref, a_ref, s0_ref, kk_ref, ka_ref, rk_ref,
                 y_ref, bon_ref, sf_ref, s_ref, qt_ref, y0_ref, gh_ref, ht_ref, et_ref, *, L, n, nb, hpb):
    d = pl.program_id(1)
    blk = pl.program_id(3)
    TB = n * L

    npair = hpb // HEADS_PER_TILE

    @pl.when(blk == 0)
    def _():
        for p in range(npair):
            s_ref[p] = jnp.concatenate([s0_ref[0, 0, HEADS_PER_TILE * p + hh] for hh in range(HEADS_PER_TILE)], axis=1)

    sign = 1 - 2 * d
    row = lax.broadcasted_iota(jnp.int32, (TB, TB), 0)
    col = lax.broadcasted_iota(jnp.int32, (TB, TB), 1)
    tri = jnp.where(((row // L) == (col // L)) & ((row - col) * sign >= 0), 1.0, 0.0).astype(BF16)
    lrow = lax.broadcasted_iota(jnp.int32, (LANES, LANES), 0)
    lcol = lax.broadcasted_iota(jnp.int32, (LANES, LANES), 1)
    head_ones = jnp.where((lrow // HEAD) == (lcol // HEAD), 1.0, 0.0).astype(BF16)
    wrow = lax.broadcasted_iota(jnp.int32, (L, LANES), 0)
    wcol = lax.broadcasted_iota(jnp.int32, (L, LANES), 1)
    gorder = (wrow - wcol % L) * sign
    strict = gorder > 0
    incl = gorder >= 0
    first = wcol < HEAD
    eye_w = jnp.where(wrow == wcol % HEAD, 1.0, 0.0)

    def seg_sum(x):
        xb = x.astype(BF16)
        tiles = [slice(t * LANES, (t + 1) * LANES) for t in range(npair)]
        return jnp.concatenate([_dot(xb[:, t], head_ones) for t in tiles], axis=1)

    def cumsum_rows(x):
        xh, xl = _split2(x)
        return _dot(tri, xh) + _dot(tri, xl)

    r2, k2, v2 = r_ref[0], k_ref[0], v_ref[0]
    lw2, a2 = lw_ref[0, 0], a_ref[0, 0]
    kkr = k2 * kk_ref[...]
    kk = kkr * lax.rsqrt(seg_sum(kkr * kkr) + 1e-12)
    kd = k2 * (1.0 + (a2 - 1.0) * ka_ref[...])
    b2 = kk * a2
    bon_ref[0, 0] = seg_sum(r2 * kd * rk_ref[...]) * v2

    cum = cumsum_rows(lw2)
    tot_rows = [jnp.where(d == 0, cum[(j + 1) * L - 1:(j + 1) * L], cum[j * L:j * L + 1]) for j in range(n)]
    tot = jnp.concatenate([jnp.broadcast_to(x, (L, x.shape[1])) for x in tot_rows], axis=0)
    al2 = -kk * jnp.exp(cum - lw2)
    rh2 = r2 * jnp.exp(cum)
    e_neg = jnp.exp(-cum)
    bc2 = b2 * e_neg
    kc2 = kd * e_neg
    e_tail = jnp.exp(tot - cum)
    be2 = b2 * e_tail
    ke2 = kd * e_tail
    for j in range(n):
        et_ref[j * SUBLANES:(j + 1) * SUBLANES, :] = jnp.broadcast_to(jnp.exp(tot_rows[j]), (SUBLANES, tot.shape[1]))

    chains = [(j, p) for j in range(n) for p in range(npair)]
    nch = range(len(chains))

    def part(x, i):
        j, p = chains[i]
        return x[j * L:(j + 1) * L, p * LANES:(p + 1) * LANES]

    def swap(x):
        return pltpu.roll(x, HEAD, 1)

    def only(x, hh):
        return jnp.where(first, x, 0.0) if hh == 0 else jnp.where(first, 0.0, x)

    def bd(x):
        return jnp.concatenate([only(x, 0), only(x, 1)], axis=0).astype(BF16)

    heads = range(HEADS_PER_TILE)
    zeros_w = jnp.zeros((L, LANES), F32)
    al = [part(al2, i) for i in nch]
    rh = [part(rh2, i) for i in nch]
    v = [part(v2, i) for i in nch]
    lhs = [jnp.concatenate([only(al[i], hh) for hh in heads] + [only(rh[i], hh) for hh in heads], axis=0).astype(BF16)
           for i in nch]
    rhs = [jnp.concatenate([part(bc2, i), part(kc2, i)], axis=0).astype(BF16) for i in nch]
    bek = [jnp.concatenate([part(be2, i), part(ke2, i)], axis=0).astype(BF16) for i in nch]
    gram = [_dot(lhs[i], rhs[i], NT) for i in nch]
    top = [[jnp.where(strict, gram[i][hh * L:(hh + 1) * L], 0.0) for hh in heads] for i in nch]
    bot = [[jnp.where(incl, gram[i][(2 + hh) * L:(3 + hh) * L], 0.0).astype(BF16) for hh in heads]
           for i in nch]
    mab = [jnp.where(first, top[i][0], swap(top[i][1])) for i in nch]
    mak = [jnp.where(first, swap(top[i][0]), top[i][1]) for i in nch]
    w = [_dot(mak[i].astype(BF16), bd(v[i])) for i in nch]
    pw = [mab[i].astype(BF16) for i in nch]
    t = [eye_w + pw[i].astype(F32) for i in nch]
    pw = [_dot(pw[i], bd(pw[i])).astype(BF16) for i in nch]
    span = 4
    while span < L:
        both = [_dot(jnp.concatenate([pw[i], t[i].astype(BF16)], axis=0), bd(pw[i])) for i in nch]
        pw = [both[i][:L].astype(BF16) for i in nch]
        t = [t[i] + both[i][L:] for i in nch]
        span *= 2
    t = [t[i] + _dot(t[i].astype(BF16), bd(pw[i])) for i in nch]
    x0 = [jnp.where(first, al[i], swap(w[i])) for i in nch]
    x1 = [jnp.where(first, swap(al[i]), w[i]) for i in nch]
    xbd = [jnp.concatenate([jnp.concatenate([x0[i], zeros_w], axis=1),
                            jnp.concatenate([zeros_w, x1[i]], axis=1)], axis=0).astype(BF16) for i in nch]
    tx = [_dot(t[i].astype(BF16), xbd[i]) for i in nch]
    vz = [[jnp.where(first, 0.0, swap(v[i])), jnp.where(first, 0.0, v[i])] for i in nch]
    z = [[jnp.concatenate([tx[i][:, hh * LANES:(hh + 1) * LANES], vz[i][hh]], axis=0).astype(BF16) for hh in heads]
         for i in nch]
    qy = [[_dot(bot[i][hh], z[i][hh]) for hh in heads] for i in nch]
    ghs = [_dot(jnp.concatenate(z[i], axis=1), bek[i], TN) for i in nch]
    gh = [[ghs[i][hh * LANES:(hh + 1) * LANES] for hh in heads] for i in nch]
    for i in nch:
        j, p = chains[i]
        qt_ref[p, j * L:(j + 1) * L, :] = rh[i] + jnp.where(first, qy[i][0], swap(qy[i][1]))
        y0_ref[p, j * L:(j + 1) * L, :] = jnp.where(first, swap(qy[i][0]), qy[i][1])
        gh_ref[p, j * LANES:(j + 1) * LANES, :] = jnp.concatenate(
            [only(gh[i][0][:HEAD], 0), only(gh[i][1][:HEAD], 1)], axis=0).astype(BF16)
        ht_ref[p, j * L:(j + 1) * L, :] = jnp.where(first, gh[i][0][HEAD:], gh[i][1][HEAD:])

    def chunk_step(jj, carry):
        cj = jj + d * (n - 1 - 2 * jj)
        r0 = pl.multiple_of(cj * L, L)
        g0 = pl.multiple_of(cj * LANES, LANES)
        e0 = pl.multiple_of(cj * SUBLANES, SUBLANES)
        s = [s_ref[p] for p in range(npair)]
        sbd = [bd(x) for x in s]
        y = [_dot(qt_ref[p, pl.ds(r0, L), :].astype(BF16), sbd[p], NT) for p in range(npair)]
        sg = [_dot(s[p].astype(BF16), gh_ref[p, pl.ds(g0, LANES), :]) for p in range(npair)]
        for p in range(npair):
            y_ref[0, 0, pl.ds(r0, L), p * LANES:(p + 1) * LANES] = y[p] + y0_ref[p, pl.ds(r0, L), :]
            decay = et_ref[pl.ds(e0, SUBLANES), p * LANES:(p + 1) * LANES][0:1, :]
            s_ref[p] = s[p] * decay + sg[p] + ht_ref[p, pl.ds(r0, L), :]
        return carry

    lax.fori_loop(0, n, chunk_step, 0, unroll=True)

    @pl.when(blk == nb - 1)
    def _():
        for p in range(npair):
            for hh in range(HEADS_PER_TILE):
                sf_ref[0, 0, HEADS_PER_TILE * p + hh] = s_ref[p][:, hh * HEAD:(hh + 1) * HEAD]


def _rwkv_scan_call(r, k, v, lw, a, s0, p, tb, hpb):
    B, T, D = r.shape
    L = SCAN_CHUNK
    n = tb // L
    nb = T // tb
    wl = hpb * HEAD
    npair = hpb // HEADS_PER_TILE

    def bidx(d, c):
        return c + d * (nb - 1 - 2 * c)

    row_spec = pl.BlockSpec((1, tb, wl), lambda b, d, h, c: (b, bidx(d, c), h))
    dir_spec = pl.BlockSpec((1, 1, tb, wl), lambda b, d, h, c: (d, b, bidx(d, c), h))
    st_spec = pl.BlockSpec((1, 1, hpb, HEAD, HEAD), lambda b, d, h, c: (b, d, h, 0, 0))
    vec_spec = pl.BlockSpec((1, wl), lambda b, d, h, c: (0, h))
    return pl.pallas_call(
        functools.partial(_scan_kernel, L=L, n=n, nb=nb, hpb=hpb),
        grid=(B, 2, HEADS // hpb, nb),
        in_specs=[row_spec, row_spec, row_spec, dir_spec, dir_spec, st_spec, vec_spec, vec_spec, vec_spec],
        out_specs=[dir_spec, dir_spec, st_spec],
        out_shape=[jax.ShapeDtypeStruct((2, B, T, D), F32), jax.ShapeDtypeStruct((2, B, T, D), F32),
                   jax.ShapeDtypeStruct((B, 2, HEADS, HEAD, HEAD), F32)],
        scratch_shapes=[pltpu.VMEM((npair, HEAD, LANES), F32),
                        pltpu.VMEM((npair, tb, LANES), F32),
                        pltpu.VMEM((npair, tb, LANES), F32),
                        pltpu.VMEM((npair, n * LANES, LANES), BF16),
                        pltpu.VMEM((npair, tb, LANES), F32),
                        pltpu.VMEM((n * SUBLANES, wl), F32)],
        compiler_params=_params("parallel", "parallel", "parallel", "arbitrary"),
        name="rwkv_scan",
    )(r, k, v, lw, a, s0, p["k_k"], p["k_a"], p["r_k"])


def _seg_mean(x, e_ref, et_ref):
    xh, xl = _split2(x)
    s = (_dot(xh, e_ref[...]) + _dot(xl, e_ref[...])) * (1.0 / HEAD)
    sh, slo = _split2(s)
    return _dot(sh, et_ref[...]) + _dot(slo, et_ref[...])


def _seg_mean_nonneg(x, e_ref, et_ref):
    s = _dot(x.astype(BF16), e_ref[...]) * (1.0 / HEAD)
    return _dot(s.astype(BF16), et_ref[...])


MLP_CHUNKS = 4


def _mlp_apply(x, mod_ref, g_ref, w1_ref, w2_ref):
    h = _norm_mod(x, g_ref[...], mod_ref[0, 3:4, :], mod_ref[0, 4:5, :]).astype(BF16)
    fc = D_FF // MLP_CHUNKS
    acc = jnp.zeros(x.shape, F32)
    for j in range(MLP_CHUNKS):
        hid = jnp.maximum(_dot(h, w1_ref[:, j * fc:(j + 1) * fc]), 0.0)
        acc = acc + _dot((hid * hid).astype(BF16), w2_ref[j * fc:(j + 1) * fc, :])
    return x + mod_ref[0, 5:6, :] * acc


def _mlp_specs():
    return [_const_spec((1, D_MODEL)), _const_spec((D_MODEL, D_FF)), _const_spec((D_FF, D_MODEL))]


def _rwkv_post_kernel(y_ref, bon_ref, gate_ref, x_ref, mod_ref, lnw_ref, lnb_ref, e_ref, et_ref, wo_ref,
                      g2_ref, w1_ref, w2_ref, o_ref):
    y = y_ref[0, 0] + y_ref[1, 0]
    yc = y - _seg_mean(y, e_ref, et_ref)
    var = _seg_mean_nonneg(yc * yc, e_ref, et_ref)
    yn = yc * lax.rsqrt(var + GN_EPS) * lnw_ref[...] + lnb_ref[...]
    o = (yn + (bon_ref[0, 0] + bon_ref[1, 0])) * gate_ref[0]
    x = x_ref[0] + mod_ref[0, 2:3, :] * _dot(o.astype(BF16), wo_ref[...])
    o_ref[0] = _mlp_apply(x, mod_ref, g2_ref, w1_ref, w2_ref)


def _rwkv_post_call(y, bon, gate, x, mod, p, seg, g2, w1, w2, tm):
    B, T, D = x.shape
    row_spec = pl.BlockSpec((1, tm, D), lambda b, i: (b, i, 0))
    dir_spec = pl.BlockSpec((2, 1, tm, D), lambda b, i: (0, b, i, 0))
    return pl.pallas_call(
        _rwkv_post_kernel,
        grid=(B, T // tm),
        in_specs=[dir_spec, dir_spec, row_spec, row_spec, pl.BlockSpec((1, 6, D), lambda b, i: (b, 0, 0)),
                  _const_spec((1, D)), _const_spec((1, D)), _const_spec((D, LANES)), _const_spec((LANES, D)),
                  _const_spec((D, D))] + _mlp_specs(),
        out_specs=row_spec,
        out_shape=jax.ShapeDtypeStruct((B, T, D), F32),
        compiler_params=_params("parallel", "parallel"),
        name="rwkv_post_mlp",
    )(y, bon, gate, x, mod, p["ln_w"], p["ln_b"], seg[0], seg[1], p["w_o"], g2, w1, w2)


def _na_qkv_kernel(x_ref, mod_ref, g_ref, w_ref, o_ref):
    h = _norm_mod(x_ref[0], g_ref[...], mod_ref[0, 0:1, :], mod_ref[0, 1:2, :]).astype(BF16)
    o_ref[0] = _dot(h, w_ref[...])


def _na_qkv_call(x, mod, g, w, tm):
    B, T, D = x.shape
    return pl.pallas_call(
        _na_qkv_kernel,
        grid=(B, T // tm),
        in_specs=[pl.BlockSpec((1, tm, D), lambda b, i: (b, i, 0)), pl.BlockSpec((1, 6, D), lambda b, i: (b, 0, 0)),
                  _const_spec((1, D)), _const_spec((D, 3 * D))],
        out_specs=pl.BlockSpec((1, tm, 3 * D), lambda b, i: (b, i, 0)),
        out_shape=jax.ShapeDtypeStruct((B, T, 3 * D), F32),
        compiler_params=_params("parallel", "parallel"),
        name="na_qkv",
    )(x, mod, g, w)


def _head_ones():
    lrow = lax.broadcasted_iota(jnp.int32, (LANES, LANES), 0)
    lcol = lax.broadcasted_iota(jnp.int32, (LANES, LANES), 1)
    return jnp.where((lrow // HEAD) == (lcol // HEAD), 1.0, 0.0).astype(BF16)


def _tile_head_norm(x, g, head_ones):
    ms = _dot((x * x).astype(BF16), head_ones) * (1.0 / HEAD)
    return x * lax.rsqrt(ms + NORM_EPS) * g


def _stack_heads(x, first_head):
    zero = jnp.zeros_like(x)
    return jnp.concatenate([jnp.where(first_head, x, zero), jnp.where(first_head, zero, x)], axis=0)


def _na_ctx_kernel(qkv_ref, qg_ref, kg_ref, *refs, layer, creates):
    o_ref, kc_ref, vc_ref = refs[-3:]
    slot = layer if creates else 0
    if creates:
        for other in range(kc_ref.shape[1]):
            if other != layer:
                kc_ref[0, other] = jnp.zeros(kc_ref.shape[2:], F32)
                vc_ref[0, other] = jnp.zeros(vc_ref.shape[2:], F32)
    T = qkv_ref.shape[1]
    head_ones = _head_ones()
    first_head = lax.broadcasted_iota(jnp.int32, (T, LANES), 1) < HEAD
    tiles = range(N_HEAD_TILES)

    def part(j, t):
        return qkv_ref[0, :, (j * N_HEAD_TILES + t) * LANES:(j * N_HEAD_TILES + t + 1) * LANES]

    q = [(_tile_head_norm(part(0, t), qg_ref[...], head_ones) * ATTN_SCALE).astype(BF16) for t in tiles]
    k = [_tile_head_norm(part(1, t), kg_ref[...], head_ones) for t in tiles]
    v = [part(2, t) for t in tiles]
    for t in tiles:
        for hh in range(HEADS_PER_TILE):
            kc_ref[0, slot, HEADS_PER_TILE * t + hh] = k[t][:, hh * HEAD:(hh + 1) * HEAD]
            vc_ref[0, slot, HEADS_PER_TILE * t + hh] = v[t][:, hh * HEAD:(hh + 1) * HEAD]
    s = [_dot(_stack_heads(q[t], first_head), k[t].astype(BF16), NT) for t in tiles]
    e = [jnp.exp(s[t] - jnp.max(s[t], axis=-1, keepdims=True)) for t in tiles]
    o = [_dot(e[t].astype(BF16), v[t].astype(BF16)) / jnp.sum(e[t], axis=-1, keepdims=True) for t in tiles]
    for t in tiles:
        o_ref[0, :, t * LANES:(t + 1) * LANES] = jnp.where(first_head, o[t][:T], o[t][T:])


def _qkv_specs(T):
    return [pl.BlockSpec((1, T, LANES), lambda b, h, part=part: (b, 0, part * N_HEAD_TILES + h)) for part in range(3)]


def _na_ctx_call(qkv, qg, kg, caches, layer, n_layers):
    B, T, _ = qkv.shape
    creates = caches is None
    if creates:
        cache_spec = pl.BlockSpec((1, n_layers, HEADS, T, HEAD), lambda b: (b, 0, 0, 0, 0))
        extra_specs, extra_args, aliases = [], (), {}
    else:
        cache_spec = pl.BlockSpec((1, 1, HEADS, T, HEAD), lambda b: (b, layer, 0, 0, 0))
        extra_specs, extra_args, aliases = [pl.BlockSpec(memory_space=pl.ANY)] * 2, tuple(caches), {3: 1, 4: 2}
    cache_shape = jax.ShapeDtypeStruct((B, n_layers, HEADS, T, HEAD), F32)
    return pl.pallas_call(
        functools.partial(_na_ctx_kernel, layer=layer, creates=creates),
        grid=(B,),
        in_specs=[pl.BlockSpec((1, T, 3 * D_MODEL), lambda b: (b, 0, 0)), _const_spec((1, LANES)),
                  _const_spec((1, LANES))] + extra_specs,
        out_specs=[pl.BlockSpec((1, T, D_MODEL), lambda b: (b, 0, 0)), cache_spec, cache_spec],
        out_shape=[jax.ShapeDtypeStruct((B, T, D_MODEL), F32), cache_shape, cache_shape],
        input_output_aliases=aliases,
        compiler_params=_params("parallel"),
        name="na_ctx",
    )(qkv, jnp.tile(qg, (1, HEADS_PER_TILE)), jnp.tile(kg, (1, HEADS_PER_TILE)), *extra_args)


def _na_lat_kernel(q_ref, k_ref, v_ref, kc_ref, vc_ref, bias_ref, qg_ref, kg_ref, o_ref, qn_ref, kn_ref, vn_ref,
                   *, rows, rg):
    head_ones = _head_ones()
    qn_ref[...] = (_tile_head_norm(q_ref[0], qg_ref[...], head_ones) * ATTN_SCALE).astype(BF16)
    kn_ref[...] = _tile_head_norm(k_ref[0], kg_ref[...], head_ones).astype(BF16)
    vn_ref[...] = v_ref[0].astype(BF16)
    win = WIN_ROWS * GRID_W
    kc2 = jnp.concatenate([kc_ref[0, 0, hh] for hh in range(HEADS_PER_TILE)], axis=1).astype(BF16)
    vc2 = jnp.concatenate([vc_ref[0, 0, hh] for hh in range(HEADS_PER_TILE)], axis=1).astype(BF16)
    first_head = lax.broadcasted_iota(jnp.int32, (GRID_W, LANES), 1) < HEAD
    nrg = range(rg)

    def row_group(gi, carry):
        q0, k0, var = [], [], []
        for rr in nrg:
            r = gi * rg + rr
            rs = jnp.clip(r - WIN_ROWS // 2, 0, rows - WIN_ROWS)
            var.append(rs - r + WIN_ROWS - 1)
            q0.append(pl.multiple_of(r * GRID_W, GRID_W))
            k0.append(pl.multiple_of(rs * GRID_W, GRID_W))
        qt = [qn_ref[pl.ds(q0[rr], GRID_W), :] for rr in nrg]
        q = [_stack_heads(qt[rr], first_head) for rr in nrg]
        s_w = [_dot(q[rr], kn_ref[pl.ds(k0[rr], win), :], NT) + bias_ref[0, var[rr]] for rr in nrg]
        s_c = [_dot(q[rr], kc2, NT) for rr in nrg]
        m = [jnp.maximum(jnp.max(s_w[rr], axis=-1, keepdims=True), jnp.max(s_c[rr], axis=-1, keepdims=True))
             for rr in nrg]
        e_w = [jnp.exp(s_w[rr] - m[rr]) for rr in nrg]
        e_c = [jnp.exp(s_c[rr] - m[rr]) for rr in nrg]
        den = [jnp.sum(e_w[rr], axis=-1, keepdims=True) + jnp.sum(e_c[rr], axis=-1, keepdims=True) for rr in nrg]
        o = [(_dot(e_w[rr].astype(BF16), vn_ref[pl.ds(k0[rr], win), :]) + _dot(e_c[rr].astype(BF16), vc2)) / den[rr]
             for rr in nrg]
        for rr in nrg:
            o_ref[0, pl.ds(q0[rr], GRID_W), :] = jnp.where(first_head, o[rr][:GRID_W], o[rr][GRID_W:])
        return carry

    lax.fori_loop(0, rows // rg, row_group, 0)


def _na_lat_call(qkv, k_ctx, v_ctx, layer, bias, qg, kg):
    B, T, _ = qkv.shape
    P = k_ctx.shape[3]
    rows = T // GRID_W
    ctx_spec = pl.BlockSpec((1, 1, HEADS_PER_TILE, P, HEAD), lambda b, h: (b, layer, h, 0, 0))
    return pl.pallas_call(
        functools.partial(_na_lat_kernel, rows=rows, rg=16),
        grid=(B, N_HEAD_TILES),
        in_specs=_qkv_specs(T) + [
            ctx_spec, ctx_spec,
            pl.BlockSpec((1, WIN_ROWS, HEADS_PER_TILE * GRID_W, WIN_ROWS * GRID_W), lambda b, h: (h, 0, 0, 0)),
            _const_spec((1, LANES)), _const_spec((1, LANES))],
        out_specs=pl.BlockSpec((1, T, LANES), lambda b, h: (b, 0, h)),
        out_shape=jax.ShapeDtypeStruct((B, T, D_MODEL), F32),
        scratch_shapes=[pltpu.VMEM((T, LANES), BF16)] * 3,
        compiler_params=_params("parallel", "parallel"),
        name="na_lat",
    )(qkv, qkv, qkv, k_ctx, v_ctx, bias, jnp.tile(qg, (1, HEADS_PER_TILE)), jnp.tile(kg, (1, HEADS_PER_TILE)))


def _na_bias_table(rpb):
    n_col = 2 * WIN_COLS - 1
    qc = np.arange(GRID_W)[:, None]
    kc = np.arange(GRID_W)[None, :]
    col = np.clip(kc - qc + WIN_COLS - 1, 0, n_col - 1)
    ws = np.clip(qc - WIN_COLS // 2, 0, GRID_W - WIN_COLS)
    valid = (kc >= ws) & (kc < ws + WIN_COLS)
    onehot = np.concatenate([(col[None] == np.arange(n_col)[:, None, None]) & valid[None], ~valid[None]], axis=0)
    rows = jnp.stack([rpb[:, al:al + WIN_ROWS, :] for al in range(WIN_ROWS)], axis=1)
    rows = jnp.concatenate([rows, jnp.full(rows.shape[:-1] + (1,), NEG_BIG, F32)], axis=-1)
    rows = rows.reshape(N_HEAD_TILES, HEADS_PER_TILE, WIN_ROWS, WIN_ROWS, n_col + 1)
    tab = jnp.einsum("thajc,cqk->tahqjk", rows, jnp.asarray(onehot.astype(np.float32)),
                     precision=lax.Precision.HIGHEST)
    return tab.reshape(N_HEAD_TILES, WIN_ROWS, HEADS_PER_TILE * GRID_W, WIN_ROWS * GRID_W)


def _out_proj_kernel(a_ref, x_ref, mod_ref, w_ref, g2_ref, w1_ref, w2_ref, o_ref):
    x = x_ref[0] + mod_ref[0, 2:3, :] * _dot(a_ref[0].astype(BF16), w_ref[...])
    o_ref[0] = _mlp_apply(x, mod_ref, g2_ref, w1_ref, w2_ref)


def _out_proj_call(a, x, mod, w, g2, w1, w2, tm):
    B, T, D = x.shape
    row_spec = pl.BlockSpec((1, tm, D), lambda b, i: (b, i, 0))
    return pl.pallas_call(
        _out_proj_kernel,
        grid=(B, T // tm),
        in_specs=[row_spec, row_spec, pl.BlockSpec((1, 6, D), lambda b, i: (b, 0, 0)), _const_spec((D, D))]
        + _mlp_specs(),
        out_specs=row_spec,
        out_shape=jax.ShapeDtypeStruct((B, T, D), F32),
        compiler_params=_params("parallel", "parallel"),
        name="out_proj_mlp",
    )(a, x, mod, w, g2, w1, w2)


def _pad_lora_in(w):
    return jnp.pad(w, ((0, 0), (0, LORA_PAD - w.shape[1])))


def _rwkv_layer_params(i, mu, w_rkv, w_o, w0, w1, w2, a0, a1, a2, g1, g2, k_k, k_a, r_k, ln_w, ln_b):
    rank = w1.shape[-1]
    wl1 = jnp.stack([jnp.concatenate([w1[i, 0], w1[i, 1]], axis=1),
                     jnp.concatenate([a1[i, 0], a1[i, 1]], axis=1),
                     _pad_lora_in(g1[i])]).astype(BF16)
    zeros = jnp.zeros((rank, D_MODEL), F32)
    w2p = jnp.stack([jnp.concatenate([w2[i, 0], zeros]), jnp.concatenate([zeros, w2[i, 1]])]).astype(BF16)
    a2p = jnp.stack([jnp.concatenate([a2[i, 0], zeros]), jnp.concatenate([zeros, a2[i, 1]])]).astype(BF16)
    g2p = jnp.pad(g2[i], ((0, LORA_PAD - g2.shape[1]), (0, 0))).astype(BF16)
    return dict(mu=mu[i], w_rkv=w_rkv[i].astype(BF16), wl1=wl1, w2p=w2p, w0=w0[i], a2p=a2p, a0=a0[i], g2=g2p,
                k_k=k_k[i][None], k_a=k_a[i][None], r_k=r_k[i].reshape(1, D_MODEL),
                ln_w=ln_w[i][None], ln_b=ln_b[i][None], w_o=w_o[i].astype(BF16))


def _segment_matrices():
    e = (np.arange(D_MODEL)[:, None] // HEAD == np.arange(LANES)[None, :]).astype(np.float32)
    return jnp.asarray(e, BF16), jnp.asarray(e.T, BF16)


def kernel(x_prompt, x_sample, state_rwkv, cache_na_k, cache_na_v, c, c_ctx, norm_g, ada_w, ada_b, mlp_w1, mlp_w2, rwkv_mu, rwkv_w_rkv, rwkv_w_o, rwkv_w0, rwkv_w1, rwkv_w2, rwkv_a0, rwkv_a1, rwkv_a2, rwkv_g1, rwkv_g2, rwkv_k_k, rwkv_k_a, rwkv_r_k, rwkv_ln_w, rwkv_ln_b, na_w_qkv, na_w_o, na_q_g, na_k_g, na_rpb):
    n_dec = c.shape[0]
    bp = x_prompt.shape[0]
    cond_rows = 16
    cond = jnp.zeros((cond_rows, D_MODEL), F32).at[:n_dec].set(c).at[n_dec].set(c_ctx)
    mods = _ada_call(cond, ada_w, ada_b)
    mod_lat = mods[:, :n_dec].reshape(DEPTH, n_dec, 6, D_MODEL)
    mod_ctx = jnp.broadcast_to(mods[:, n_dec].reshape(DEPTH, 1, 6, D_MODEL), (DEPTH, bp, 6, D_MODEL))

    rwkv_raw = (rwkv_mu, rwkv_w_rkv, rwkv_w_o, rwkv_w0, rwkv_w1, rwkv_w2, rwkv_a0, rwkv_a1, rwkv_a2,
                rwkv_g1, rwkv_g2, rwkv_k_k, rwkv_k_a, rwkv_r_k, rwkv_ln_w, rwkv_ln_b)
    n_rwkv = rwkv_mu.shape[0]
    n_na = na_w_qkv.shape[0]
    rwkv_p = [_rwkv_layer_params(i, *rwkv_raw) for i in range(n_rwkv)]
    seg = _segment_matrices()
    w1_bf = [mlp_w1[l].astype(BF16) for l in range(DEPTH)]
    w2_bf = [mlp_w2[l].astype(BF16) for l in range(DEPTH)]
    wqkv_bf = [na_w_qkv[i].astype(BF16) for i in range(n_na)]
    wo_bf = [na_w_o[i].astype(BF16) for i in range(n_na)]
    bias_tabs = [_na_bias_table(na_rpb[i]) for i in range(n_na)]

    def run(x, mod_all, tm, s0_fn, attn_fn):
        states = []
        for l in range(DEPTH):
            i = l // 2
            mod = mod_all[l]
            mlp = (norm_g[l, 1][None], w1_bf[l], w2_bf[l])
            if l % 2 == 0:
                p = rwkv_p[i]
                r, k, v, gate, lw, a = _rwkv_proj_call(x, mod, norm_g[l, 0][None], p, tm)
                y, bon, s_fin = _rwkv_scan_call(r, k, v, lw, a, s0_fn(i), p, 256, 16)
                states.append(s_fin)
                x = _rwkv_post_call(y, bon, gate, x, mod, p, seg, *mlp, 256)
            else:
                qkv = _na_qkv_call(x, mod, norm_g[l, 0][None], wqkv_bf[i], tm)
                o = attn_fn(i, qkv)
                x = _out_proj_call(o, x, mod, wo_bf[i], *mlp, tm)
        return x, states

    new_kv = []

    def ctx_attn(i, qkv):
        o, k_c, v_c = _na_ctx_call(qkv, na_q_g[i][None], na_k_g[i][None], new_kv[-1] if new_kv else None, i, n_na)
        new_kv.append((k_c, v_c))
        return o

    def lat_attn(i, qkv):
        return _na_lat_call(qkv, cache_na_k, cache_na_v, i, bias_tabs[i], na_q_g[i][None], na_k_g[i][None])

    zero_state = jnp.zeros((bp, 2, HEADS, HEAD, HEAD), F32)
    y_prompt, new_states = run(x_prompt, mod_ctx, 256, lambda i: zero_state, ctx_attn)
    y_sample, _ = run(x_sample, mod_lat, 512, lambda i: state_rwkv[:, i], lat_attn)
    return (y_prompt, y_sample, jnp.stack(new_states, axis=1), new_kv[-1][0], new_kv[-1][1])
```
